```python
import math
import jax, jax.numpy as jnp
from jax import lax
import numpy as np

D_MODEL = 2048
BATCH = 16
SEQ = 2048
DEPTH = 2

D_MIX = D_MODEL
DA_WIDTH = D_MIX // 2
DA_HEADS = 8
DA_HEAD_V = DA_WIDTH // DA_HEADS
DA_HEAD_QK = DA_HEAD_V // 2
GLA_WIDTH = D_MIX - DA_WIDTH
GLA_HEADS = 4
GLA_KEY_DIM = GLA_WIDTH // 2
GLA_HEAD_K = GLA_KEY_DIM // GLA_HEADS
GLA_HEAD_V = GLA_WIDTH // GLA_HEADS
GLA_GATE_RANK = 16
GLA_GATE_NORMALIZER = 16.0
GLA_CHUNK = 64
Q_BLOCK = 128
D_FF = ((8 * D_MODEL // 3 + 255) // 256) * 256
CONV_WIDTH = 3
NORM_EPS = 1e-6
N_MOD = 6
PROJ_SIZES = (DA_WIDTH, DA_WIDTH, DA_WIDTH,
              GLA_KEY_DIM, GLA_KEY_DIM, GLA_WIDTH,
              GLA_WIDTH, GLA_GATE_RANK)
N_PROJ = sum(PROJ_SIZES)
PROJ_SPLITS = tuple(int(s) for s in np.cumsum(PROJ_SIZES)[:-1])

kernel_name = "hybrid_diffattn_gla_convffn_adaln"


def _alibi_slopes(n):
    start = 2.0 ** (-8.0 / n)
    return np.array([start ** (i + 1) for i in range(n)], dtype=np.float32)


def _lambda_init(layer_idx):
    return 0.8 - 0.6 * math.exp(-0.3 * layer_idx)


def rms_norm(x, g):
    xf = x.astype(jnp.float32)
    y = xf * lax.rsqrt(jnp.mean(xf * xf, axis=-1, keepdims=True) + NORM_EPS)
    return (y * g.astype(jnp.float32)).astype(x.dtype)


def diff_attention(q, k, v, lam, slopes):
    T = q.shape[3]
    scale = DA_HEAD_QK ** -0.5
    outs = []
    for i in range(T // Q_BLOCK):
        q0, q1 = i * Q_BLOCK, (i + 1) * Q_BLOCK
        qb = q[:, :, :, q0:q1]
        kb = k[:, :, :, :q1]
        vb = v[:, :, :q1]
        s = jnp.einsum('bhmqd,bhmkd->bhmqk', qb, kb).astype(jnp.float32) * scale
        dist = (jnp.arange(q0, q1)[:, None] - jnp.arange(q1)[None, :]).astype(jnp.float32)
        bias = jnp.where(dist[None] >= 0, -slopes[:, None, None] * dist[None], -jnp.inf)
        p = jax.nn.softmax(s + bias[None, :, None], axis=-1)
        pd = p[:, :, 0] - lam * p[:, :, 1]
        outs.append(jnp.einsum('bhqk,bhkd->bhqd', pd.astype(vb.dtype), vb))
    return jnp.concatenate(outs, axis=2)


def gla_chunked(q, k, v, log_a):
    B, H, T, dk = q.shape
    dv = v.shape[-1]
    nc = T // GLA_CHUNK

    def to_chunks(a):
        return jnp.moveaxis(a.reshape(B, H, nc, GLA_CHUNK, a.shape[-1]), 2, 0)

    qc, kc, vc = (to_chunks(a.astype(jnp.float32)) for a in (q, k, v))
    bc = jnp.cumsum(to_chunks(log_a.astype(jnp.float32)), axis=-2)
    causal = jnp.tril(jnp.ones((GLA_CHUNK, GLA_CHUNK), dtype=bool))

    def step(S, inp):
        qi, ki, vi, bi = inp
        diff = bi[:, :, :, None, :] - bi[:, :, None, :, :]
        decay = jnp.exp(jnp.where(causal[None, None, :, :, None], diff, -jnp.inf))
        A = jnp.einsum('bhtsd,bhsd->bhts', qi[:, :, :, None, :] * decay, ki)
        o = jnp.einsum('bhts,bhsv->bhtv', A, vi) \
            + jnp.einsum('bhtd,bhdv->bhtv', qi * jnp.exp(bi), S)
        b_last = bi[:, :, -1:, :]
        S_new = jnp.exp(b_last[:, :, 0, :])[..., None] * S \
            + jnp.einsum('bhsd,bhsv->bhdv', ki * jnp.exp(b_last - bi), vi)
        return S_new, o

    S0 = jnp.zeros((B, H, dk, dv), jnp.float32)
    _, oc = lax.scan(step, S0, (qc, kc, vc, bc))
    return jnp.moveaxis(oc, 0, 2).reshape(B, H, T, dv)


def causal_dwconv(x, w, b):
    T = x.shape[1]
    xp = jnp.pad(x, ((0, 0), (CONV_WIDTH - 1, 0), (0, 0)))
    y = b.astype(x.dtype)
    for j in range(CONV_WIDTH):
        y = y + xp[:, j:j + T] * w[j].astype(x.dtype)
    return y


def setup_inputs(seed: int = 0) -> dict:
    key = jax.random.key(seed)
    ks = jax.random.split(key, 20)
    n = jax.random.normal
    f = jnp.float32
    return {
        "x": n(ks[0], (BATCH, SEQ, D_MODEL), f),
        "c": n(ks[1], (BATCH, D_MODEL), f),
        "w_ada": n(ks[2], (DEPTH, D_MODEL, N_MOD * D_MODEL), f) * D_MODEL ** -0.5,
        "b_ada": n(ks[3], (DEPTH, N_MOD * D_MODEL), f) * 0.02,
        "g_mix_norm": 1.0 + 0.02 * n(ks[4], (DEPTH, D_MODEL), f),
        "w_in": n(ks[5], (DEPTH, D_MODEL, N_PROJ), f) * D_MODEL ** -0.5,
        "diff_lambda": 0.1 * n(ks[6], (DEPTH, 4, DA_HEAD_QK), f),
        "g_diff_subln": 1.0 + 0.02 * n(ks[7], (DEPTH, DA_HEAD_V), f),
        "w_gk_up": n(ks[8], (DEPTH, GLA_GATE_RANK, GLA_KEY_DIM), f) * GLA_GATE_RANK ** -0.5,
        "b_gk": 0.02 * n(ks[9], (DEPTH, GLA_KEY_DIM), f),
        "g_gla_norm": 1.0 + 0.02 * n(ks[10], (DEPTH, GLA_HEAD_V), f),
        "w_o": n(ks[11], (DEPTH, D_MIX, D_MODEL), f) * D_MIX ** -0.5,
        "g_ffn_norm": 1.0 + 0.02 * n(ks[12], (DEPTH, D_MODEL), f),
        "w_up": n(ks[13], (DEPTH, D_MODEL, 2 * D_FF), f) * D_MODEL ** -0.5,
        "w_conv": n(ks[14], (DEPTH, CONV_WIDTH, D_FF), f) * CONV_WIDTH ** -0.5,
        "b_conv": 0.02 * n(ks[15], (DEPTH, D_FF), f),
        "w_down": n(ks[16], (DEPTH, D_FF, D_MODEL), f) * D_FF ** -0.5,
        "g_final": 1.0 + 0.02 * n(ks[17], (D_MODEL,), f),
    }


def reference(x, c, w_ada, b_ada, g_mix_norm, w_in, diff_lambda, g_diff_subln, w_gk_up, b_gk,
              g_gla_norm, w_o, g_ffn_norm, w_up, w_conv, b_conv, w_down, g_final):
    B, T, _ = x.shape
    slopes = jnp.asarray(_alibi_slopes(DA_HEADS))
    c_act = jax.nn.silu(c)
    for l in range(DEPTH):
        mod = (c_act @ w_ada[l] + b_ada[l])[:, None, :]
        sh1, sc1, gt1, sh2, sc2, gt2 = jnp.split(mod, N_MOD, axis=-1)

        h = rms_norm(x, g_mix_norm[l]) * (1.0 + sc1) + sh1
        proj = h @ w_in[l]
        dq, dk, dv, gq, gk, gv, gg, g_low = jnp.split(proj, PROJ_SPLITS, axis=-1)

        lam_init = _lambda_init(l)
        lq1, lk1, lq2, lk2 = (diff_lambda[l, i].astype(jnp.float32) for i in range(4))
        lam = jnp.exp(jnp.sum(lq1 * lk1)) - jnp.exp(jnp.sum(lq2 * lk2)) + lam_init
        qd = dq.reshape(B, T, DA_HEADS, 2, DA_HEAD_QK).transpose(0, 2, 3, 1, 4)
        kd = dk.reshape(B, T, DA_HEADS, 2, DA_HEAD_QK).transpose(0, 2, 3, 1, 4)
        vd = dv.reshape(B, T, DA_HEADS, DA_HEAD_V).transpose(0, 2, 1, 3)
        od = diff_attention(qd, kd, vd, lam, slopes)
        od = rms_norm(od, g_diff_subln[l]) * (1.0 - lam_init)
        od = od.transpose(0, 2, 1, 3).reshape(B, T, DA_WIDTH)

        log_a = jax.nn.log_sigmoid((g_low @ w_gk_up[l] + b_gk[l]).astype(jnp.float32)) / GLA_GATE_NORMALIZER
        qg = (gq * GLA_HEAD_K ** -0.5).reshape(B, T, GLA_HEADS, GLA_HEAD_K).transpose(0, 2, 1, 3)
        kg = gk.reshape(B, T, GLA_HEADS, GLA_HEAD_K).transpose(0, 2, 1, 3)
        vg = gv.reshape(B, T, GLA_HEADS, GLA_HEAD_V).transpose(0, 2, 1, 3)
        ag = log_a.reshape(B, T, GLA_HEADS, GLA_HEAD_K).transpose(0, 2, 1, 3)
        og = gla_chunked(qg, kg, vg, ag).astype(x.dtype).transpose(0, 2, 1, 3)
        og = rms_norm(og, g_gla_norm[l]) * jax.nn.silu(gg.reshape(B, T, GLA_HEADS, GLA_HEAD_V))
        og = og.reshape(B, T, GLA_WIDTH)

        mix = jnp.concatenate([od, og], axis=-1) @ w_o[l]
        x = x + gt1 * mix

        h2 = rms_norm(x, g_ffn_norm[l]) * (1.0 + sc2) + sh2
        u, g = jnp.split(h2 @ w_up[l], 2, axis=-1)
        act = jax.nn.gelu(causal_dwconv(g, w_conv[l], b_conv[l]), approximate=False) * u
        x = x + gt2 * (act @ w_down[l])
    return rms_norm(x, g_final)
```

```python
import functools
import math

import numpy as np
import jax
import jax.numpy as jnp
from jax import lax
from jax.experimental import pallas as pl
from jax.experimental.pallas import tpu as pltpu

BF16 = jnp.bfloat16
F32 = jnp.float32

NORM_EPS = 1e-6
N_MOD = 6
DA_HEADS = 8
DA_HEAD_V = 128
DA_HEAD_QK = 64
GLA_HEADS = 4
GLA_HEAD_K = 128
GLA_HEAD_V = 256
GLA_GATE_RANK = 16
GLA_GATE_NORMALIZER = 16.0
CONV_WIDTH = 3

LANE = 128
V7X_VMEM_LIMIT = 56 * 1024 * 1024
LOG2E = 1.4426950408889634
SQRT_HALF = 0.7071067811865476

_NT = (((1,), (1,)), ((), ()))
_TN = (((0,), (0,)), ((), ()))


def _cparams(sem):
    return pltpu.CompilerParams(dimension_semantics=sem, vmem_limit_bytes=V7X_VMEM_LIMIT)


def _lambda_init(layer_idx):
    return 0.8 - 0.6 * math.exp(-0.3 * layer_idx)


def _rms(x, g):
    ms = jnp.mean(x * x, axis=-1, keepdims=True)
    return x * lax.rsqrt(ms + NORM_EPS) * g


def _mod_kernel(c_ref, w_ref, b_ref, o_ref):
    c = c_ref[...]
    ca = (c * jax.nn.sigmoid(c)).astype(BF16)
    w = w_ref[0].astype(BF16)
    o_ref[0] = jnp.dot(ca, w, preferred_element_type=F32) + b_ref[0]


def _modulation(c, w_ada, b_ada, tn=1024):
    depth, d, n = w_ada.shape
    b = c.shape[0]
    return pl.pallas_call(
        _mod_kernel,
        out_shape=jax.ShapeDtypeStruct((depth, b, n), F32),
        grid=(depth, n // tn),
        in_specs=[
            pl.BlockSpec((b, d), lambda l, j: (0, 0)),
            pl.BlockSpec((1, d, tn), lambda l, j: (l, 0, j)),
            pl.BlockSpec((1, 1, tn), lambda l, j: (l, 0, j)),
        ],
        out_specs=pl.BlockSpec((1, b, tn), lambda l, j: (l, 0, j)),
        compiler_params=_cparams(("parallel", "parallel")),
        name="adaln_mod",
    )(c, w_ada, b_ada.reshape(depth, 1, n))


def _inproj_kernel(x_ref, mod_ref, g_ref, w_ref, o_ref, h_ref):
    @pl.when(pl.program_id(1) == 0)
    def _():
        y = _rms(x_ref[...], g_ref[...])
        h_ref[...] = (y * (1.0 + mod_ref[0, 1:2, :]) + mod_ref[0, 0:1, :]).astype(BF16)

    o_ref[...] = jnp.dot(h_ref[...], w_ref[...], preferred_element_type=F32).astype(BF16)


def _in_projection(x2d, mod, g, w, seq, tm=1024, tn=1280):
    n, d = x2d.shape
    p = w.shape[1]
    tm = min(tm, seq)
    tiles_per_seq = seq // tm
    return pl.pallas_call(
        _inproj_kernel,
        out_shape=jax.ShapeDtypeStruct((n, p), BF16),
        grid=(n // tm, p // tn),
        in_specs=[
            pl.BlockSpec((tm, d), lambda i, j: (i, 0)),
            pl.BlockSpec((1, N_MOD, d), lambda i, j: (i // tiles_per_seq, 0, 0)),
            pl.BlockSpec((1, d), lambda i, j: (0, 0)),
            pl.BlockSpec((d, tn), lambda i, j: (0, j)),
        ],
        out_specs=pl.BlockSpec((tm, tn), lambda i, j: (i, j)),
        scratch_shapes=[pltpu.VMEM((tm, d), BF16)],
        compiler_params=_cparams(("parallel", "arbitrary")),
        name="norm_in_proj",
    )(x2d, mod, g.reshape(1, d), w)


def _dattn_kernel(slopes_ref, q_ref, k_ref, v_ref, lamp_ref, gsub_ref, o_ref, *, blk, lam_init):
    h = pl.program_id(1)
    i = pl.program_id(2)
    slope = slopes_ref[h]

    q = q_ref[0]
    lane = lax.broadcasted_iota(jnp.int32, q.shape, 1)
    qs = q * jnp.asarray(DA_HEAD_QK ** -0.5, BF16)
    zero = jnp.zeros_like(qs)
    q1 = jnp.where(lane < DA_HEAD_QK, qs, zero)
    q2 = jnp.where(lane >= DA_HEAD_QK, qs, zero)

    ii = lax.broadcasted_iota(jnp.int32, (blk, blk), 0)
    jj = lax.broadcasted_iota(jnp.int32, (blk, blk), 1)
    rel = (ii - jj).astype(F32)
    cmat = -slope * rel

    def attend(qm, kb, vb, bias, m, l, a):
        s = lax.dot_general(qm, kb, _NT, preferred_element_type=F32) + bias
        mn = jnp.maximum(m, jnp.max(s, axis=-1, keepdims=True))
        p = jnp.exp(s - mn)
        alpha = jnp.exp(m - mn)
        l = alpha * l + jnp.sum(p, axis=-1, keepdims=True)
        a = alpha * a + jnp.dot(p.astype(BF16), vb, preferred_element_type=F32)
        return mn, l, a

    def step(j, carry, bias):
        m1, l1, a1, m2, l2, a2 = carry
        start = pl.multiple_of(j * blk, blk)
        kb = k_ref[0, pl.ds(start, blk), :]
        vb = v_ref[0, pl.ds(start, blk), :]
        m1, l1, a1 = attend(q1, kb, vb, bias, m1, l1, a1)
        m2, l2, a2 = attend(q2, kb, vb, bias, m2, l2, a2)
        return m1, l1, a1, m2, l2, a2

    def off_diag(j, carry):
        c0 = -slope * ((i - j) * blk).astype(F32)
        return step(j, carry, cmat + c0)

    m0 = jnp.full((blk, 1), -1e30, F32)
    l0 = jnp.zeros((blk, 1), F32)
    a0 = jnp.zeros((blk, DA_HEAD_V), F32)
    carry = lax.fori_loop(0, i, off_diag, (m0, l0, a0, m0, l0, a0))
    diag_bias = jnp.where(rel >= 0, cmat, -jnp.inf)
    m1, l1, a1, m2, l2, a2 = step(i, carry, diag_bias)

    lamp = lamp_ref[...]
    s1 = jnp.sum(lamp[0:1] * lamp[1:2], axis=-1, keepdims=True)
    s2 = jnp.sum(lamp[2:3] * lamp[3:4], axis=-1, keepdims=True)
    lam = jnp.exp(s1) - jnp.exp(s2) + lam_init
    o = a1 * (1.0 / l1) - lam * (a2 * (1.0 / l2))
    o = _rms(o, gsub_ref[...]) * (1.0 - lam_init)
    o_ref[0] = o.astype(BF16)


def _diff_attention(proj, slopes, lamp, gsub, layer_idx, blk=256):
    b, t, _ = proj.shape
    kern = functools.partial(_dattn_kernel, blk=blk, lam_init=_lambda_init(layer_idx))
    return pl.pallas_call(
        kern,
        out_shape=jax.ShapeDtypeStruct((b, t, DA_HEADS * DA_HEAD_V), BF16),
        grid_spec=pltpu.PrefetchScalarGridSpec(
            num_scalar_prefetch=1,
            grid=(b, DA_HEADS, t // blk),
            in_specs=[
                pl.BlockSpec((1, blk, LANE), lambda bi, h, i, s: (bi, i, h)),
                pl.BlockSpec((1, t, LANE), lambda bi, h, i, s: (bi, 0, DA_HEADS + h)),
                pl.BlockSpec((1, t, LANE), lambda bi, h, i, s: (bi, 0, 2 * DA_HEADS + h)),
                pl.BlockSpec((4, DA_HEAD_QK), lambda bi, h, i, s: (0, 0)),
                pl.BlockSpec((1, DA_HEAD_V), lambda bi, h, i, s: (0, 0)),
            ],
            out_specs=pl.BlockSpec((1, blk, LANE), lambda bi, h, i, s: (bi, i, h)),
        ),
        compiler_params=_cparams(("parallel", "parallel", "parallel")),
        name="diff_attention",
    )(slopes, proj, proj, proj, lamp, gsub.reshape(1, DA_HEAD_V))


def _gla_cum_matrices(c):
    nlev = int(math.log2(c))
    m = np.zeros((nlev + 2, c, c), np.float32)
    for lev in range(nlev):
        h = c >> (lev + 1)
        for r in range(c):
            mid = (r // (2 * h)) * 2 * h + h
            if r % (2 * h) >= h:
                m[lev, r, mid:r + 1] = 1.0
            else:
                m[lev, r, r + 1:mid] = 1.0
    for r in range(c):
        m[nlev, r, :r + 1] = 1.0
        m[nlev + 1, r, r + 1:] = 1.0
    return jnp.asarray(m, BF16)


def _gla_kernel(gq_ref, gk_ref, gv_ref, gg_ref, glow_ref, wup_ref, bgk_ref, gnorm_ref, cum_ref,
                o_ref, s_ref, qt_ref, kt_ref, *, c):
    nlev = int(math.log2(c))
    dk, dv, nh = GLA_HEAD_K, GLA_HEAD_V, GLA_HEADS

    @pl.when(pl.program_id(1) == 0)
    def _():
        s_ref[...] = jnp.zeros_like(s_ref)

    x = jnp.dot(glow_ref[0], wup_ref[...], preferred_element_type=F32) + bgk_ref[...]
    logsig = jnp.minimum(x, 0.0) - jnp.log(1.0 + jnp.exp(-jnp.abs(x)))
    la = logsig * (LOG2E / GLA_GATE_NORMALIZER)
    la_hi = la.astype(BF16)
    la_mid = (la - la_hi.astype(F32)).astype(BF16)

    def cum(idx):
        mat = cum_ref[idx]
        return (jnp.dot(mat, la_hi, preferred_element_type=F32)
                + jnp.dot(mat, la_mid, preferred_element_type=F32))

    q = gq_ref[0].astype(F32) * (dk ** -0.5)
    k = gk_ref[0].astype(F32)
    row = lax.broadcasted_iota(jnp.int32, (c, 1), 0)

    for lev in range(nlev):
        p = jnp.exp2(cum(lev))
        tside = ((row >> (nlev - 1 - lev)) & 1) == 1
        qt_ref[lev] = (q * jnp.where(tside, p, 0.0)).astype(BF16)
        kt_ref[lev] = (k * jnp.where(tside, 0.0, p)).astype(BF16)

    qb = (q * jnp.exp2(cum(nlev))).astype(BF16)
    kr = (k * jnp.exp2(cum(nlev + 1))).astype(BF16)
    qk = q * k

    ti = lax.broadcasted_iota(jnp.int32, (c, c), 0)
    si = lax.broadcasted_iota(jnp.int32, (c, c), 1)
    ones = jnp.ones((c, LANE), BF16)

    for hh in range(nh):
        ks = slice(hh * dk, (hh + 1) * dk)
        vs = slice(hh * dv, (hh + 1) * dv)
        a = lax.dot_general(qt_ref[0, :, ks], kt_ref[0, :, ks], _NT, preferred_element_type=F32)
        for lev in range(1, nlev):
            shift = nlev - lev
            same_group = (ti >> shift) == (si >> shift)
            al = lax.dot_general(qt_ref[lev, :, ks], kt_ref[lev, :, ks], _NT,
                                 preferred_element_type=F32)
            a = jnp.where(same_group, al, a)
        v = gv_ref[0, :, vs]
        s_old = s_ref[hh]
        diag = jnp.sum(qk[:, ks], axis=-1, keepdims=True)
        o = (jnp.dot(a.astype(BF16), v, preferred_element_type=F32)
             + jnp.dot(qb[:, ks], s_old.astype(BF16), preferred_element_type=F32)
             + diag * v.astype(F32))
        tot = (lax.dot_general(la_hi[:, ks], ones, _TN, preferred_element_type=F32)
               + lax.dot_general(la_mid[:, ks], ones, _TN, preferred_element_type=F32))
        dec = jnp.exp2(tot)
        dec = jnp.concatenate([dec] * (dv // LANE), axis=1)
        s_ref[hh] = dec * s_old + lax.dot_general(kr[:, ks], v, _TN, preferred_element_type=F32)
        on = _rms(o, gnorm_ref[...])
        g = gg_ref[0, :, vs].astype(F32)
        o_ref[0, :, vs] = (on * (g * jax.nn.sigmoid(g))).astype(BF16)


def _gla(proj, wup, bgk, gnorm, c=256):
    b, t, _ = proj.shape
    nlev = int(math.log2(c))
    kw = GLA_HEADS * GLA_HEAD_K
    vw = GLA_HEADS * GLA_HEAD_V
    q_off = 3 * DA_HEADS * DA_HEAD_V
    cum = _gla_cum_matrices(c)
    kern = functools.partial(_gla_kernel, c=c)
    return pl.pallas_call(
        kern,
        out_shape=jax.ShapeDtypeStruct((b, t, vw), BF16),
        grid=(b, t // c),
        in_specs=[
            pl.BlockSpec((1, c, kw), lambda bi, i: (bi, i, q_off // kw)),
            pl.BlockSpec((1, c, kw), lambda bi, i: (bi, i, q_off // kw + 1)),
            pl.BlockSpec((1, c, vw), lambda bi, i: (bi, i, (q_off + 2 * kw) // vw)),
            pl.BlockSpec((1, c, vw), lambda bi, i: (bi, i, (q_off + 2 * kw) // vw + 1)),
            pl.BlockSpec((1, c, LANE), lambda bi, i: (bi, i, (q_off + 2 * kw + 2 * vw) // LANE)),
            pl.BlockSpec((LANE, kw), lambda bi, i: (0, 0)),
            pl.BlockSpec((1, kw), lambda bi, i: (0, 0)),
            pl.BlockSpec((1, GLA_HEAD_V), lambda bi, i: (0, 0)),
            pl.BlockSpec((nlev + 2, c, c), lambda bi, i: (0, 0, 0)),
        ],
        out_specs=pl.BlockSpec((1, c, vw), lambda bi, i: (bi, i, 0)),
        scratch_shapes=[
            pltpu.VMEM((GLA_HEADS, GLA_HEAD_K, GLA_HEAD_V), F32),
            pltpu.VMEM((nlev, c, kw), BF16),
            pltpu.VMEM((nlev, c, kw), BF16),
        ],
        compiler_params=_cparams(("parallel", "arbitrary")),
        name="gla",
    )(proj, proj, proj, proj, proj, wup, bgk.reshape(1, kw), gnorm.reshape(1, GLA_HEAD_V), cum)


def _outproj_kernel(od_ref, og_ref, w1_ref, w2_ref, x_ref, mod_ref, g_ref, x1_ref, h2_ref):
    mix = (jnp.dot(od_ref[...], w1_ref[...], preferred_element_type=F32)
           + jnp.dot(og_ref[...], w2_ref[...], preferred_element_type=F32))
    x1 = x_ref[...] + mod_ref[0, 2:3, :] * mix
    x1_ref[...] = x1
    y = _rms(x1, g_ref[...])
    h2_ref[...] = (y * (1.0 + mod_ref[0, 4:5, :]) + mod_ref[0, 3:4, :]).astype(BF16)


def _out_projection(od, og, w_o, x2d, mod, g, seq, tm=512):
    n, d = x2d.shape
    half = od.shape[1]
    tiles_per_seq = seq // tm
    return pl.pallas_call(
        _outproj_kernel,
        out_shape=(jax.ShapeDtypeStruct((n, d), F32), jax.ShapeDtypeStruct((n, d), BF16)),
        grid=(n // tm,),
        in_specs=[
            pl.BlockSpec((tm, half), lambda i: (i, 0)),
            pl.BlockSpec((tm, half), lambda i: (i, 0)),
            pl.BlockSpec((half, d), lambda i: (0, 0)),
            pl.BlockSpec((half, d), lambda i: (1, 0)),
            pl.BlockSpec((tm, d), lambda i: (i, 0)),
            pl.BlockSpec((1, N_MOD, d), lambda i: (i // tiles_per_seq, 0, 0)),
            pl.BlockSpec((1, d), lambda i: (0, 0)),
        ],
        out_specs=(pl.BlockSpec((tm, d), lambda i: (i, 0)), pl.BlockSpec((tm, d), lambda i: (i, 0))),
        compiler_params=_cparams(("parallel",)),
        name="out_proj_residual_norm",
    )(od, og, w_o, w_o, x2d, mod, g.reshape(1, d))


def _ffn_kernel(h2_ref, wu_ref, wg_ref, wc_ref, bc_ref, wd_ref, x1_ref, mod_ref, gfin_ref,
                o_ref, acc_ref, carry_ref, *, tiles_per_seq, final):
    i = pl.program_id(0)
    j = pl.program_id(1)
    tm = h2_ref.shape[0]
    tf = wu_ref.shape[1]
    sub = 8

    h2 = h2_ref[...]
    u = jnp.dot(h2, wu_ref[...], preferred_element_type=F32)
    g = jnp.dot(h2, wg_ref[...], preferred_element_type=F32)

    prev = carry_ref[j]
    prev = jnp.where(i % tiles_per_seq == 0, jnp.zeros_like(prev), prev)
    carry_ref[j] = g[tm - sub:, :]

    r1 = pltpu.roll(g, 1, 0)
    r2 = pltpu.roll(g, 2, 0)
    row = lax.broadcasted_iota(jnp.int32, (sub, tf), 0)
    head1 = jnp.where(row == 0, prev[sub - 1:sub], r1[:sub])
    head2 = jnp.where(row == 0, prev[sub - 2:sub - 1], jnp.where(row == 1, prev[sub - 1:sub], r2[:sub]))
    g1 = jnp.concatenate([head1, r1[sub:]], axis=0)
    g2 = jnp.concatenate([head2, r2[sub:]], axis=0)

    wc = wc_ref[...]
    y = bc_ref[...] + g2 * wc[0:1] + g1 * wc[1:2] + g * wc[2:3]
    act = (0.5 * y * (1.0 + lax.erf(y * SQRT_HALF))) * u
    part = jnp.dot(act.astype(BF16), wd_ref[...], preferred_element_type=F32)

    @pl.when(j == 0)
    def _():
        acc_ref[...] = part

    @pl.when(j > 0)
    def _():
        acc_ref[...] += part

    @pl.when(j == pl.num_programs(1) - 1)
    def _():
        x2 = x1_ref[...] + mod_ref[0, 5:6, :] * acc_ref[...]
        if final:
            x2 = _rms(x2, gfin_ref[...])
        o_ref[...] = x2


def _ffn(h2, w_up, w_conv, b_conv, w_down, x1, mod, g_final, seq, final, tm=512, tf=512):
    n, d = x1.shape
    f = w_down.shape[0]
    nf = f // tf
    tiles_per_seq = seq // tm
    kern = functools.partial(_ffn_kernel, tiles_per_seq=tiles_per_seq, final=final)
    return pl.pallas_call(
        kern,
        out_shape=jax.ShapeDtypeStruct((n, d), F32),
        grid=(n // tm, nf),
        in_specs=[
            pl.BlockSpec((tm, d), lambda i, j: (i, 0)),
            pl.BlockSpec((d, tf), lambda i, j: (0, j)),
            pl.BlockSpec((d, tf), lambda i, j: (0, nf + j)),
            pl.BlockSpec((CONV_WIDTH, tf), lambda i, j: (0, j)),
            pl.BlockSpec((1, tf), lambda i, j: (0, j)),
            pl.BlockSpec((tf, d), lambda i, j: (j, 0)),
            pl.BlockSpec((tm, d), lambda i, j: (i, 0)),
            pl.BlockSpec((1, N_MOD, d), lambda i, j: (i // tiles_per_seq, 0, 0)),
            pl.BlockSpec((1, d), lambda i, j: (0, 0)),
        ],
        out_specs=pl.BlockSpec((tm, d), lambda i, j: (i, 0)),
        scratch_shapes=[
            pltpu.VMEM((tm, d), F32),
            pltpu.VMEM((nf, 8, tf), F32),
        ],
        compiler_params=_cparams(("arbitrary", "arbitrary")),
        name="conv_ffn_residual",
    )(h2, w_up, w_up, w_conv, b_conv.reshape(1, f), w_down, x1, mod, g_final.reshape(1, d))


def _alibi_slopes(n):
    start = 2.0 ** (-8.0 / n)
    return np.array([start ** (i + 1) for i in range(n)], dtype=np.float32)


def kernel(x, c, w_ada, b_ada, g_mix_norm, w_in, diff_lambda, g_diff_subln, w_gk_up, b_gk,
           g_gla_norm, w_o, g_ffn_norm, w_up, w_conv, b_conv, w_down, g_final):
    b, t, d = x.shape
    depth = w_ada.shape[0]
    n_proj = w_in.shape[2]
    in_tile = 1280
    n_proj_pad = -(-n_proj // in_tile) * in_tile
    slopes = jnp.asarray(_alibi_slopes(DA_HEADS))

    mod_all = _modulation(c, w_ada, b_ada).reshape(depth, b, N_MOD, d)
    x2d = x.reshape(b * t, d)
    for l in range(depth):
        mod = mod_all[l]
        w_in_l = jnp.pad(w_in[l].astype(BF16), ((0, 0), (0, n_proj_pad - n_proj)))
        wup_l = jnp.pad(w_gk_up[l].astype(BF16), ((0, LANE - GLA_GATE_RANK), (0, 0)))

        proj = _in_projection(x2d, mod, g_mix_norm[l], w_in_l, t, tn=in_tile).reshape(b, t, n_proj_pad)
        od = _diff_attention(proj, slopes, diff_lambda[l], g_diff_subln[l], l)
        og = _gla(proj, wup_l, b_gk[l], g_gla_norm[l])
        x1, h2 = _out_projection(od.reshape(b * t, -1), og.reshape(b * t, -1), w_o[l].astype(BF16),
                                 x2d, mod, g_ffn_norm[l], t)
        x2d = _ffn(h2, w_up[l].astype(BF16), w_conv[l], b_conv[l], w_down[l].astype(BF16), x1, mod,
                   g_final, t, final=(l == depth - 1))
    return x2d.reshape(b, t, d)
```

```python
import functools
import math

import numpy as np
import jax
import jax.numpy as jnp
from jax import lax
from jax.experimental import pallas as pl
from jax.experimental.pallas import tpu as pltpu

BF16 = jnp.bfloat16
F32 = jnp.float32

NORM_EPS = 1e-6
N_MOD = 6
DA_HEADS = 8
DA_HEAD_V = 128
DA_HEAD_QK = 64
GLA_HEADS = 4
GLA_HEAD_K = 128
GLA_HEAD_V = 256
GLA_GATE_RANK = 16
GLA_GATE_NORMALIZER = 16.0
CONV_WIDTH = 3

LANE = 128
V7X_VMEM_LIMIT = 56 * 1024 * 1024
LOG2E = 1.4426950408889634
SQRT_HALF = 0.7071067811865476

_NT = (((1,), (1,)), ((), ()))
_TN = (((0,), (0,)), ((), ()))


def _cparams(sem):
    return pltpu.CompilerParams(dimension_semantics=sem, vmem_limit_bytes=V7X_VMEM_LIMIT)


def _lambda_init(layer_idx):
    return 0.8 - 0.6 * math.exp(-0.3 * layer_idx)


def _rms(x, g):
    ms = jnp.mean(x * x, axis=-1, keepdims=True)
    return x * lax.rsqrt(ms + NORM_EPS) * g


def _mod_kernel(c_ref, w_ref, b_ref, o_ref):
    c = c_ref[...]
    ca = (c * jax.nn.sigmoid(c)).astype(BF16)
    w = w_ref[0].astype(BF16)
    o_ref[0] = jnp.dot(ca, w, preferred_element_type=F32) + b_ref[0]


def _modulation(c, w_ada, b_ada, tn=1024):
    depth, d, n = w_ada.shape
    b = c.shape[0]
    return pl.pallas_call(
        _mod_kernel,
        out_shape=jax.ShapeDtypeStruct((depth, b, n), F32),
        grid=(depth, n // tn),
        in_specs=[
            pl.BlockSpec((b, d), lambda l, j: (0, 0)),
            pl.BlockSpec((1, d, tn), lambda l, j: (l, 0, j)),
            pl.BlockSpec((1, 1, tn), lambda l, j: (l, 0, j)),
        ],
        out_specs=pl.BlockSpec((1, b, tn), lambda l, j: (l, 0, j)),
        compiler_params=_cparams(("parallel", "parallel")),
        name="adaln_mod",
    )(c, w_ada, b_ada.reshape(depth, 1, n))


def _inproj_kernel(x_ref, mod_ref, g_ref, w_ref, cs_ref, o_ref, h_ref):
    @pl.when(pl.program_id(1) == 0)
    def _():
        y = _rms(x_ref[...], g_ref[...])
        h_ref[...] = (y * (1.0 + mod_ref[0, 1:2, :]) + mod_ref[0, 0:1, :]).astype(BF16)

    o_ref[...] = (jnp.dot(h_ref[...], w_ref[...], preferred_element_type=F32) * cs_ref[...]).astype(BF16)


def _in_projection(x2d, mod, g, w, col_scale, seq, tm=1024, tn=1280):
    n, d = x2d.shape
    p = w.shape[1]
    tm = min(tm, seq)
    tiles_per_seq = seq // tm
    return pl.pallas_call(
        _inproj_kernel,
        out_shape=jax.ShapeDtypeStruct((n, p), BF16),
        grid=(n // tm, p // tn),
        in_specs=[
            pl.BlockSpec((tm, d), lambda i, j: (i, 0)),
            pl.BlockSpec((1, N_MOD, d), lambda i, j: (i // tiles_per_seq, 0, 0)),
            pl.BlockSpec((1, d), lambda i, j: (0, 0)),
            pl.BlockSpec((d, tn), lambda i, j: (0, j)),
            pl.BlockSpec((1, tn), lambda i, j: (0, j)),
        ],
        out_specs=pl.BlockSpec((tm, tn), lambda i, j: (i, j)),
        scratch_shapes=[pltpu.VMEM((tm, d), BF16)],
        compiler_params=_cparams(("parallel", "arbitrary")),
        name="norm_in_proj",
    )(x2d, mod, g.reshape(1, d), w, col_scale)


def _alibi_slopes(n):
    start = 2.0 ** (-8.0 / n)
    return np.array([start ** (i + 1) for i in range(n)], dtype=np.float32)


def _bf16_split(x):
    hi = x.astype(BF16).astype(np.float32)
    lo = (x - hi).astype(BF16).astype(np.float32)
    return hi, lo


def _alibi_aug(t, blk):
    j = np.arange(t)
    kaug = np.zeros((t, LANE), np.float32)
    kaug[:, 0] = kaug[:, 2] = j % 256
    kaug[:, 1] = kaug[:, 3] = j // 256
    kaug[:, 4] = 1.0
    c = _alibi_slopes(DA_HEADS).astype(np.float64) * LOG2E
    c_hi, c_lo = _bf16_split(c.astype(np.float32))
    qaug = np.zeros((DA_HEADS, t // blk, LANE), np.float32)
    qaug[:, :, 0] = c_hi[:, None]
    qaug[:, :, 1] = 256.0 * c_hi[:, None]
    qaug[:, :, 2] = c_lo[:, None]
    qaug[:, :, 3] = 256.0 * c_lo[:, None]
    qaug[:, :, 4] = -(c[:, None] * (np.arange(t // blk)[None, :] * blk))
    return jnp.asarray(kaug.astype(BF16)), jnp.asarray(qaug.astype(BF16))


def _dattn_kernel(q_ref, k_ref, v_ref, kaug_ref, qaug_ref, lamp_ref, gsub_ref, o_ref, *, blk, lam_init):
    t = q_ref.shape[1]
    nq = t // blk
    q = q_ref[0]
    kk = jnp.concatenate([k_ref[0], kaug_ref[...]], axis=1)
    va = jnp.concatenate([v_ref[0], jnp.ones((t, LANE), BF16)], axis=1)

    lamp = lamp_ref[...]
    s1 = jnp.sum(lamp[0:1] * lamp[1:2], axis=-1, keepdims=True)
    s2 = jnp.sum(lamp[2:3] * lamp[3:4], axis=-1, keepdims=True)
    lam = jnp.exp(s1) - jnp.exp(s2) + lam_init

    lane = lax.broadcasted_iota(jnp.int32, (blk, LANE), 1)
    ri = lax.broadcasted_iota(jnp.int32, (2 * blk, blk), 0)
    ci = lax.broadcasted_iota(jnp.int32, (2 * blk, blk), 1)
    causal = (ri & (blk - 1)) >= ci
    zero = jnp.zeros((blk, LANE), BF16)

    for i in range(nq):
        qi = q[i * blk:(i + 1) * blk]
        aug = jnp.broadcast_to(qaug_ref[0, i:i + 1, :], (blk, LANE))
        q1 = jnp.concatenate([jnp.where(lane < DA_HEAD_QK, qi, zero), aug], axis=1)
        q2 = jnp.concatenate([jnp.where(lane >= DA_HEAD_QK, qi, zero), aug], axis=1)
        qc = jnp.concatenate([q1, q2], axis=0)
        lo = i * blk
        sd = lax.dot_general(qc, kk[lo:lo + blk], _NT, preferred_element_type=F32)
        sd = jnp.where(causal, sd, -jnp.inf)
        m = jnp.max(sd, axis=-1, keepdims=True)
        if i > 0:
            sm = lax.dot_general(qc, kk[:lo], _NT, preferred_element_type=F32)
            m = jnp.maximum(m, jnp.max(sm, axis=-1, keepdims=True))
        o = jnp.dot(jnp.exp2(sd - m).astype(BF16), va[lo:lo + blk], preferred_element_type=F32)
        if i > 0:
            o = o + jnp.dot(jnp.exp2(sm - m).astype(BF16), va[:lo], preferred_element_type=F32)
        on = o[:, :DA_HEAD_V] * (1.0 / o[:, DA_HEAD_V:DA_HEAD_V + 1])
        od = on[:blk] - lam * on[blk:]
        od = _rms(od, gsub_ref[...]) * (1.0 - lam_init)
        o_ref[0, i * blk:(i + 1) * blk, :] = od.astype(BF16)


def _diff_attention(proj, lamp, gsub, layer_idx, blk=256):
    b, t, _ = proj.shape
    kaug, qaug = _alibi_aug(t, blk)
    kern = functools.partial(_dattn_kernel, blk=blk, lam_init=_lambda_init(layer_idx))
    return pl.pallas_call(
        kern,
        out_shape=jax.ShapeDtypeStruct((b, t, DA_HEADS * DA_HEAD_V), BF16),
        grid=(b, DA_HEADS),
        in_specs=[
            pl.BlockSpec((1, t, LANE), lambda bi, h: (bi, 0, h)),
            pl.BlockSpec((1, t, LANE), lambda bi, h: (bi, 0, DA_HEADS + h)),
            pl.BlockSpec((1, t, LANE), lambda bi, h: (bi, 0, 2 * DA_HEADS + h)),
            pl.BlockSpec((t, LANE), lambda bi, h: (0, 0)),
            pl.BlockSpec((1, t // blk, LANE), lambda bi, h: (h, 0, 0)),
            pl.BlockSpec((4, DA_HEAD_QK), lambda bi, h: (0, 0)),
            pl.BlockSpec((1, DA_HEAD_V), lambda bi, h: (0, 0)),
        ],
        out_specs=pl.BlockSpec((1, t, LANE), lambda bi, h: (bi, 0, h)),
        compiler_params=_cparams(("parallel", "parallel")),
        name="diff_attention",
    )(proj, proj, proj, kaug, qaug, lamp, gsub.reshape(1, DA_HEAD_V))


def _gla_cum_matrices(c):
    nlev = int(math.log2(c))
    m = np.zeros((nlev + 2, c, c), np.float32)
    for lev in range(nlev):
        h = c >> (lev + 1)
        for r in range(c):
            mid = (r // (2 * h)) * 2 * h + h
            if r % (2 * h) >= h:
                m[lev, r, mid:r + 1] = 1.0
            else:
                m[lev, r, r + 1:mid] = 1.0
    for r in range(c):
        m[nlev, r, :r + 1] = 1.0
        m[nlev + 1, r, r + 1:] = 1.0
    return jnp.asarray(m.astype(BF16))


def _gla_kernel(gq_ref, gk_ref, gv_ref, gg_ref, glow_ref, wup_ref, bgk_ref, gnorm_ref, cum_ref,
                o_ref, s_ref, qt_ref, kt_ref, *, c):
    nlev = int(math.log2(c))
    dk, dv, nh = GLA_HEAD_K, GLA_HEAD_V, GLA_HEADS

    @pl.when(pl.program_id(1) == 0)
    def _():
        s_ref[...] = jnp.zeros_like(s_ref)

    x = jnp.dot(glow_ref[0], wup_ref[...], preferred_element_type=F32) + bgk_ref[...]
    logsig = jnp.minimum(x, 0.0) - jnp.log(1.0 + jnp.exp(-jnp.abs(x)))
    la = logsig * (LOG2E / GLA_GATE_NORMALIZER)
    la_hi = la.astype(BF16)
    la_mid = (la - la_hi.astype(F32)).astype(BF16)

    def cum(idx):
        mat = cum_ref[idx]
        return (jnp.dot(mat, la_hi, preferred_element_type=F32)
                + jnp.dot(mat, la_mid, preferred_element_type=F32))

    q = gq_ref[0].astype(F32)
    k = gk_ref[0].astype(F32)
    row = lax.broadcasted_iota(jnp.int32, (c, 1), 0)

    for lev in range(nlev):
        p = jnp.exp2(cum(lev))
        tside = ((row >> (nlev - 1 - lev)) & 1) == 1
        qt_ref[lev] = (q * jnp.where(tside, p, 0.0)).astype(BF16)
        kt_ref[lev] = (k * jnp.where(tside, 0.0, p)).astype(BF16)

    qb = (q * jnp.exp2(cum(nlev))).astype(BF16)
    kr = (k * jnp.exp2(cum(nlev + 1))).astype(BF16)
    qk = q * k

    ti = lax.broadcasted_iota(jnp.int32, (c, c), 0)
    si = lax.broadcasted_iota(jnp.int32, (c, c), 1)
    ones = jnp.ones((c, LANE), BF16)

    for hh in range(nh):
        ks = slice(hh * dk, (hh + 1) * dk)
        vs = slice(hh * dv, (hh + 1) * dv)
        a = lax.dot_general(qt_ref[0, :, ks], kt_ref[0, :, ks], _NT, preferred_element_type=F32)
        for lev in range(1, nlev):
            shift = nlev - lev
            same_group = (ti >> shift) == (si >> shift)
            al = lax.dot_general(qt_ref[lev, :, ks], kt_ref[lev, :, ks], _NT,
                                 preferred_element_type=F32)
            a = jnp.where(same_group, al, a)
        v = gv_ref[0, :, vs]
        s_old = s_ref[hh]
        diag = jnp.sum(qk[:, ks], axis=-1, keepdims=True)
        o = (jnp.dot(a.astype(BF16), v, preferred_element_type=F32)
             + jnp.dot(qb[:, ks], s_old.astype(BF16), preferred_element_type=F32)
             + diag * v.astype(F32))
        tot = (lax.dot_general(la_hi[:, ks], ones, _TN, preferred_element_type=F32)
               + lax.dot_general(la_mid[:, ks], ones, _TN, preferred_element_type=F32))
        dec = jnp.exp2(tot)
        dec = jnp.concatenate([dec] * (dv // LANE), axis=1)
        s_ref[hh] = dec * s_old + lax.dot_general(kr[:, ks], v, _TN, preferred_element_type=F32)
        on = _rms(o, gnorm_ref[...])
        g = gg_ref[0, :, vs].astype(F32)
        o_ref[0, :, vs] = (on * (g * jax.nn.sigmoid(g))).astype(BF16)


def _gla(proj, wup, bgk, gnorm, c=256):
    b, t, _ = proj.shape
    nlev = int(math.log2(c))
    kw = GLA_HEADS * GLA_HEAD_K
    vw = GLA_HEADS * GLA_HEAD_V
    q_off = 3 * DA_HEADS * DA_HEAD_V
    cum = _gla_cum_matrices(c)
    kern = functools.partial(_gla_kernel, c=c)
    return pl.pallas_call(
        kern,
        out_shape=jax.ShapeDtypeStruct((b, t, vw), BF16),
        grid=(b, t // c),
        in_specs=[
            pl.BlockSpec((1, c, kw), lambda bi, i: (bi, i, q_off // kw)),
            pl.BlockSpec((1, c, kw), lambda bi, i: (bi, i, q_off // kw + 1)),
            pl.BlockSpec((1, c, vw), lambda bi, i: (bi, i, (q_off + 2 * kw) // vw)),
            pl.BlockSpec((1, c, vw), lambda bi, i: (bi, i, (q_off + 2 * kw) // vw + 1)),
            pl.BlockSpec((1, c, LANE), lambda bi, i: (bi, i, (q_off + 2 * kw + 2 * vw) // LANE)),
            pl.BlockSpec((LANE, kw), lambda bi, i: (0, 0)),
            pl.BlockSpec((1, kw), lambda bi, i: (0, 0)),
            pl.BlockSpec((1, GLA_HEAD_V), lambda bi, i: (0, 0)),
            pl.BlockSpec((nlev + 2, c, c), lambda bi, i: (0, 0, 0)),
        ],
        out_specs=pl.BlockSpec((1, c, vw), lambda bi, i: (bi, i, 0)),
        scratch_shapes=[
            pltpu.VMEM((GLA_HEADS, GLA_HEAD_K, GLA_HEAD_V), F32),
            pltpu.VMEM((nlev, c, kw), BF16),
            pltpu.VMEM((nlev, c, kw), BF16),
        ],
        compiler_params=_cparams(("parallel", "arbitrary")),
        name="gla",
    )(proj, proj, proj, proj, proj, wup, bgk.reshape(1, kw), gnorm.reshape(1, GLA_HEAD_V), cum)


def _outproj_kernel(od_ref, og_ref, w1_ref, w2_ref, x_ref, mod_ref, g_ref, x1_ref, h2_ref):
    mix = (jnp.dot(od_ref[...], w1_ref[...], preferred_element_type=F32)
           + jnp.dot(og_ref[...], w2_ref[...], preferred_element_type=F32))
    x1 = x_ref[...] + mod_ref[0, 2:3, :] * mix
    x1_ref[...] = x1
    y = _rms(x1, g_ref[...])
    h2_ref[...] = (y * (1.0 + mod_ref[0, 4:5, :]) + mod_ref[0, 3:4, :]).astype(BF16)


def _out_projection(od, og, w_o, x2d, mod, g, seq, tm=512):
    n, d = x2d.shape
    half = od.shape[1]
    tiles_per_seq = seq // tm
    return pl.pallas_call(
        _outproj_kernel,
        out_shape=(jax.ShapeDtypeStruct((n, d), F32), jax.ShapeDtypeStruct((n, d), BF16)),
        grid=(n // tm,),
        in_specs=[
            pl.BlockSpec((tm, half), lambda i: (i, 0)),
            pl.BlockSpec((tm, half), lambda i: (i, 0)),
            pl.BlockSpec((half, d), lambda i: (0, 0)),
            pl.BlockSpec((half, d), lambda i: (1, 0)),
            pl.BlockSpec((tm, d), lambda i: (i, 0)),
            pl.BlockSpec((1, N_MOD, d), lambda i: (i // tiles_per_seq, 0, 0)),
            pl.BlockSpec((1, d), lambda i: (0, 0)),
        ],
        out_specs=(pl.BlockSpec((tm, d), lambda i: (i, 0)), pl.BlockSpec((tm, d), lambda i: (i, 0))),
        compiler_params=_cparams(("parallel",)),
        name="out_proj_residual_norm",
    )(od, og, w_o, w_o, x2d, mod, g.reshape(1, d))


def _ffn_kernel(h2_ref, wu_ref, wg_ref, wc_ref, bc_ref, wd_ref, x1_ref, mod_ref, gfin_ref,
                o_ref, acc_ref, carry_ref, *, tiles_per_seq, final):
    i = pl.program_id(0)
    j = pl.program_id(1)
    tm = h2_ref.shape[0]
    tf = wu_ref.shape[1]
    sub = 8

    h2 = h2_ref[...]
    u = jnp.dot(h2, wu_ref[...], preferred_element_type=F32)
    g = jnp.dot(h2, wg_ref[...], preferred_element_type=F32)

    prev = carry_ref[j]
    prev = jnp.where(i % tiles_per_seq == 0, jnp.zeros_like(prev), prev)
    carry_ref[j] = g[tm - sub:, :]

    r1 = pltpu.roll(g, 1, 0)
    r2 = pltpu.roll(g, 2, 0)
    row = lax.broadcasted_iota(jnp.int32, (sub, tf), 0)
    head1 = jnp.where(row == 0, prev[sub - 1:sub], r1[:sub])
    head2 = jnp.where(row == 0, prev[sub - 2:sub - 1], jnp.where(row == 1, prev[sub - 1:sub], r2[:sub]))
    g1 = jnp.concatenate([head1, r1[sub:]], axis=0)
    g2 = jnp.concatenate([head2, r2[sub:]], axis=0)

    wc = wc_ref[...]
    y = bc_ref[...] + g2 * wc[0:1] + g1 * wc[1:2] + g * wc[2:3]
    act = (0.5 * y * (1.0 + lax.erf(y * SQRT_HALF))) * u
    part = jnp.dot(act.astype(BF16), wd_ref[...], preferred_element_type=F32)

    @pl.when(j == 0)
    def _():
        acc_ref[...] = part

    @pl.when(j > 0)
    def _():
        acc_ref[...] += part

    @pl.when(j == pl.num_programs(1) - 1)
    def _():
        x2 = x1_ref[...] + mod_ref[0, 5:6, :] * acc_ref[...]
        if final:
            x2 = _rms(x2, gfin_ref[...])
        o_ref[...] = x2


def _ffn(h2, w_up, w_conv, b_conv, w_down, x1, mod, g_final, seq, final, tm=512, tf=512):
    n, d = x1.shape
    f = w_down.shape[0]
    nf = f // tf
    tiles_per_seq = seq // tm
    kern = functools.partial(_ffn_kernel, tiles_per_seq=tiles_per_seq, final=final)
    return pl.pallas_call(
        kern,
        out_shape=jax.ShapeDtypeStruct((n, d), F32),
        grid=(n // tm, nf),
        in_specs=[
            pl.BlockSpec((tm, d), lambda i, j: (i, 0)),
            pl.BlockSpec((d, tf), lambda i, j: (0, j)),
            pl.BlockSpec((d, tf), lambda i, j: (0, nf + j)),
            pl.BlockSpec((CONV_WIDTH, tf), lambda i, j: (0, j)),
            pl.BlockSpec((1, tf), lambda i, j: (0, j)),
            pl.BlockSpec((tf, d), lambda i, j: (j, 0)),
            pl.BlockSpec((tm, d), lambda i, j: (i, 0)),
            pl.BlockSpec((1, N_MOD, d), lambda i, j: (i // tiles_per_seq, 0, 0)),
            pl.BlockSpec((1, d), lambda i, j: (0, 0)),
        ],
        out_specs=pl.BlockSpec((tm, d), lambda i, j: (i, 0)),
        scratch_shapes=[
            pltpu.VMEM((tm, d), F32),
            pltpu.VMEM((nf, 8, tf), F32),
        ],
        compiler_params=_cparams(("arbitrary", "arbitrary")),
        name="conv_ffn_residual",
    )(h2, w_up, w_up, w_conv, b_conv.reshape(1, f), w_down, x1, mod, g_final.reshape(1, d))


def kernel(x, c, w_ada, b_ada, g_mix_norm, w_in, diff_lambda, g_diff_subln, w_gk_up, b_gk,
           g_gla_norm, w_o, g_ffn_norm, w_up, w_conv, b_conv, w_down, g_final):
    b, t, d = x.shape
    depth = w_ada.shape[0]
    n_proj = w_in.shape[2]
    in_tile = 1280
    n_proj_pad = -(-n_proj // in_tile) * in_tile
    col_scale = np.ones((1, n_proj_pad), np.float32)
    da_w = DA_HEADS * DA_HEAD_V
    col_scale[0, :da_w] = DA_HEAD_QK ** -0.5 * LOG2E
    col_scale[0, 3 * da_w:3 * da_w + GLA_HEADS * GLA_HEAD_K] = GLA_HEAD_K ** -0.5
    col_scale = jnp.asarray(col_scale)

    mod_all = _modulation(c, w_ada, b_ada).reshape(depth, b, N_MOD, d)
    x2d = x.reshape(b * t, d)
    for l in range(depth):
        mod = mod_all[l]
        w_in_l = jnp.pad(w_in[l].astype(BF16), ((0, 0), (0, n_proj_pad - n_proj)))
        wup_l = jnp.pad(w_gk_up[l].astype(BF16), ((0, LANE - GLA_GATE_RANK), (0, 0)))

        proj = _in_projection(x2d, mod, g_mix_norm[l], w_in_l, col_scale, t, tn=in_tile)
        proj = proj.reshape(b, t, n_proj_pad)
        od = _diff_attention(proj, diff_lambda[l], g_diff_subln[l], l)
        og = _gla(proj, wup_l, b_gk[l], g_gla_norm[l])
        x1, h2 = _out_projection(od.reshape(b * t, -1), og.reshape(b * t, -1), w_o[l].astype(BF16),
                                 x2d, mod, g_ffn_norm[l], t)
        x2d = _ffn(h2, w_up[l].astype(BF16), w_conv[l], b_conv[l], w_down[l].astype(BF16), x1, mod,
                   g_final, t, final=(l == depth - 1))
    return x2d.reshape(b, t, d)
```

```python
import functools
import math

import numpy as np
import jax
import jax.numpy as jnp
from jax import lax
from jax.experimental import pallas as pl
from jax.experimental.pallas import tpu as pltpu

BF16 = jnp.bfloat16
F32 = jnp.float32

NORM_EPS = 1e-6
N_MOD = 6
DA_HEADS = 8
DA_HEAD_V = 128
DA_HEAD_QK = 64
GLA_HEADS = 4
GLA_HEAD_K = 128
GLA_HEAD_V = 256
GLA_GATE_RANK = 16
GLA_GATE_NORMALIZER = 16.0
CONV_WIDTH = 3

LANE = 128
V7X_VMEM_LIMIT = 56 * 1024 * 1024
LOG2E = 1.4426950408889634
SQRT_HALF = 0.7071067811865476

_NT = (((1,), (1,)), ((), ()))
_TN = (((0,), (0,)), ((), ()))


def _cparams(sem):
    return pltpu.CompilerParams(dimension_semantics=sem, vmem_limit_bytes=V7X_VMEM_LIMIT)


def _lambda_init(layer_idx):
    return 0.8 - 0.6 * math.exp(-0.3 * layer_idx)


def _rms(x, g):
    ms = jnp.mean(x * x, axis=-1, keepdims=True)
    return x * lax.rsqrt(ms + NORM_EPS) * g


def _mod_kernel(c_ref, w_ref, b_ref, o_ref):
    c = c_ref[...]
    ca = (c * jax.nn.sigmoid(c)).astype(BF16)
    w = w_ref[0].astype(BF16)
    o_ref[0] = jnp.dot(ca, w, preferred_element_type=F32) + b_ref[0]


def _modulation(c, w_ada, b_ada, tn=1024):
    depth, d, n = w_ada.shape
    b = c.shape[0]
    return pl.pallas_call(
        _mod_kernel,
        out_shape=jax.ShapeDtypeStruct((depth, b, n), F32),
        grid=(depth, n // tn),
        in_specs=[
            pl.BlockSpec((b, d), lambda l, j: (0, 0)),
            pl.BlockSpec((1, d, tn), lambda l, j: (l, 0, j)),
            pl.BlockSpec((1, 1, tn), lambda l, j: (l, 0, j)),
        ],
        out_specs=pl.BlockSpec((1, b, tn), lambda l, j: (l, 0, j)),
        compiler_params=_cparams(("parallel", "parallel")),
        name="adaln_mod",
    )(c, w_ada, b_ada.reshape(depth, 1, n))


def _inproj_kernel(x_ref, mod_ref, g_ref, w_ref, cs_ref, o_ref, h_ref, *, nchunk):
    i = pl.program_id(0)
    j = pl.program_id(1)
    cr = x_ref.shape[0] // nchunk

    def norm_chunk():
        c = jnp.minimum(j, nchunk - 1)
        rows = pl.ds(pl.multiple_of(c * cr, cr), cr)
        y = _rms(x_ref[rows, :], g_ref[...])
        h_ref[i % 2, rows, :] = (y * (1.0 + mod_ref[0, 1:2, :]) + mod_ref[0, 0:1, :]).astype(BF16)

    @pl.when(i == 0)
    def _():
        norm_chunk()

    @pl.when(i > 0)
    def _():
        acc = jnp.dot(h_ref[(i + 1) % 2], w_ref[...], preferred_element_type=F32)
        o_ref[...] = (acc * cs_ref[...]).astype(BF16)
        norm_chunk()


def _in_projection(x2d, mod, g, w, col_scale, seq, tm=1024, tn=1280):
    n, d = x2d.shape
    p = w.shape[1]
    tm = min(tm, seq)
    tiles_per_seq = seq // tm
    nt = n // tm
    nj = p // tn
    nchunk = max(c for c in range(1, nj + 1) if tm % (16 * c) == 0)
    col = lambda i, j: jnp.where(i == 0, 0, j)
    return pl.pallas_call(
        functools.partial(_inproj_kernel, nchunk=nchunk),
        out_shape=jax.ShapeDtypeStruct((n, p), BF16),
        grid=(nt + 1, nj),
        in_specs=[
            pl.BlockSpec((tm, d), lambda i, j: (jnp.minimum(i, nt - 1), 0)),
            pl.BlockSpec((1, N_MOD, d), lambda i, j: (jnp.minimum(i, nt - 1) // tiles_per_seq, 0, 0)),
            pl.BlockSpec((1, d), lambda i, j: (0, 0)),
            pl.BlockSpec((d, tn), lambda i, j: (0, col(i, j))),
            pl.BlockSpec((1, tn), lambda i, j: (0, col(i, j))),
        ],
        out_specs=pl.BlockSpec((tm, tn), lambda i, j: (jnp.maximum(i - 1, 0), col(i, j))),
        scratch_shapes=[pltpu.VMEM((2, tm, d), BF16)],
        compiler_params=_cparams(("arbitrary", "arbitrary")),
        name="norm_in_proj",
    )(x2d, mod, g.reshape(1, d), w, col_scale)


def _alibi_slopes(n):
    start = 2.0 ** (-8.0 / n)
    return np.array([start ** (i + 1) for i in range(n)], dtype=np.float32)


def _bf16_split(x):
    hi = x.astype(BF16).astype(np.float32)
    lo = (x - hi).astype(BF16).astype(np.float32)
    return hi, lo


def _alibi_aug(t, blk):
    j = np.arange(t)
    kaug = np.zeros((t, LANE), np.float32)
    kaug[:, 0] = kaug[:, 2] = j % 256
    kaug[:, 1] = kaug[:, 3] = j // 256
    kaug[:, 4] = 1.0
    c = _alibi_slopes(DA_HEADS).astype(np.float64) * LOG2E
    c_hi, c_lo = _bf16_split(c.astype(np.float32))
    qaug = np.zeros((DA_HEADS, t // blk, LANE), np.float32)
    qaug[:, :, 0] = c_hi[:, None]
    qaug[:, :, 1] = 256.0 * c_hi[:, None]
    qaug[:, :, 2] = c_lo[:, None]
    qaug[:, :, 3] = 256.0 * c_lo[:, None]
    qaug[:, :, 4] = -(c[:, None] * (np.arange(t // blk)[None, :] * blk))
    return jnp.asarray(kaug.astype(BF16)), jnp.asarray(qaug.astype(BF16))


def _dattn_kernel(q_ref, k_ref, v_ref, kaug_ref, qaug_ref, lamp_ref, gsub_ref, o_ref, *, blk, lam_init):
    t = q_ref.shape[1]
    nq = t // blk
    heads = q_ref.shape[2] // LANE
    kaug = kaug_ref[...]
    ones = jnp.ones((t, LANE), BF16)

    lamp = lamp_ref[...]
    s1 = jnp.sum(lamp[0:1] * lamp[1:2], axis=-1, keepdims=True)
    s2 = jnp.sum(lamp[2:3] * lamp[3:4], axis=-1, keepdims=True)
    lam = jnp.exp(s1) - jnp.exp(s2) + lam_init

    lane = lax.broadcasted_iota(jnp.int32, (blk, LANE), 1)
    ri = lax.broadcasted_iota(jnp.int32, (2 * blk, blk), 0)
    ci = lax.broadcasted_iota(jnp.int32, (2 * blk, blk), 1)
    causal = (ri & (blk - 1)) >= ci
    zero = jnp.zeros((blk, LANE), BF16)

    def scores(h, i):
        hs = slice(h * LANE, (h + 1) * LANE)
        qi = q_ref[0, i * blk:(i + 1) * blk, hs]
        aug = jnp.broadcast_to(qaug_ref[h, i:i + 1, :], (blk, LANE))
        q1 = jnp.concatenate([jnp.where(lane < DA_HEAD_QK, qi, zero), aug], axis=1)
        q2 = jnp.concatenate([jnp.where(lane >= DA_HEAD_QK, qi, zero), aug], axis=1)
        qc = jnp.concatenate([q1, q2], axis=0)
        lo = i * blk
        kd = jnp.concatenate([k_ref[0, lo:lo + blk, hs], kaug[lo:lo + blk]], axis=1)
        sd = lax.dot_general(qc, kd, _NT, preferred_element_type=F32)
        sm = None
        if i > 0:
            km = jnp.concatenate([k_ref[0, :lo, hs], kaug[:lo]], axis=1)
            sm = lax.dot_general(qc, km, _NT, preferred_element_type=F32)
        return sd, sm

    def finish(h, i, sd, sm):
        hs = slice(h * LANE, (h + 1) * LANE)
        lo = i * blk
        sd = jnp.where(causal, sd, -jnp.inf)
        m = jnp.max(sd, axis=-1, keepdims=True)
        if i > 0:
            m = jnp.maximum(m, jnp.max(sm, axis=-1, keepdims=True))
        vd = jnp.concatenate([v_ref[0, lo:lo + blk, hs], ones[:blk]], axis=1)
        o = jnp.dot(jnp.exp2(sd - m).astype(BF16), vd, preferred_element_type=F32)
        if i > 0:
            vm = jnp.concatenate([v_ref[0, :lo, hs], ones[:lo]], axis=1)
            o = o + jnp.dot(jnp.exp2(sm - m).astype(BF16), vm, preferred_element_type=F32)
        on = o[:, :DA_HEAD_V] * (1.0 / o[:, DA_HEAD_V:DA_HEAD_V + 1])
        od = on[:blk] - lam * on[blk:]
        od = _rms(od, gsub_ref[...]) * (1.0 - lam_init)
        o_ref[0, lo:lo + blk, hs] = od.astype(BF16)

    work = [(h, i) for h in range(heads) for i in range(nq)]
    pending = scores(*work[0])
    for n, hi in enumerate(work):
        nxt = scores(*work[n + 1]) if n + 1 < len(work) else None
        finish(*hi, *pending)
        pending = nxt


def _diff_attention(proj, lamp, gsub, layer_idx, blk=256, heads_per_step=2):
    b, t, _ = proj.shape
    kaug, qaug = _alibi_aug(t, blk)
    hps = heads_per_step
    w = hps * LANE
    ngrp = DA_HEADS // hps
    kern = functools.partial(_dattn_kernel, blk=blk, lam_init=_lambda_init(layer_idx))
    return pl.pallas_call(
        kern,
        out_shape=jax.ShapeDtypeStruct((b, t, DA_HEADS * DA_HEAD_V), BF16),
        grid=(b, ngrp),
        in_specs=[
            pl.BlockSpec((1, t, w), lambda bi, h: (bi, 0, h)),
            pl.BlockSpec((1, t, w), lambda bi, h: (bi, 0, ngrp + h)),
            pl.BlockSpec((1, t, w), lambda bi, h: (bi, 0, 2 * ngrp + h)),
            pl.BlockSpec((t, LANE), lambda bi, h: (0, 0)),
            pl.BlockSpec((hps, t // blk, LANE), lambda bi, h: (h, 0, 0)),
            pl.BlockSpec((4, DA_HEAD_QK), lambda bi, h: (0, 0)),
            pl.BlockSpec((1, DA_HEAD_V), lambda bi, h: (0, 0)),
        ],
        out_specs=pl.BlockSpec((1, t, w), lambda bi, h: (bi, 0, h)),
        compiler_params=_cparams(("parallel", "parallel")),
        name="diff_attention",
    )(proj, proj, proj, kaug, qaug, lamp, gsub.reshape(1, DA_HEAD_V))


def _gla_cum_matrices(c):
    nlev = int(math.log2(c))
    m = np.zeros((nlev + 2, c, c), np.float32)
    for lev in range(nlev):
        h = c >> (lev + 1)
        for r in range(c):
            mid = (r // (2 * h)) * 2 * h + h
            if r % (2 * h) >= h:
                m[lev, r, mid:r + 1] = 1.0
            else:
                m[lev, r, r + 1:mid] = 1.0
    for r in range(c):
        m[nlev, r, :r + 1] = 1.0
        m[nlev + 1, r, r + 1:] = 1.0
    return jnp.asarray(m.astype(BF16))


def _gla_kernel(gq_ref, gk_ref, gv_ref, gg_ref, glow_ref, wup_ref, bgk_ref, gnorm_ref, cum_ref,
                o_ref, s_ref, qt_ref, kt_ref, *, c):
    nlev = int(math.log2(c))
    dk, dv, nh = GLA_HEAD_K, GLA_HEAD_V, GLA_HEADS

    @pl.when(pl.program_id(1) == 0)
    def _():
        s_ref[...] = jnp.zeros_like(s_ref)

    x = jnp.dot(glow_ref[0], wup_ref[...], preferred_element_type=F32) + bgk_ref[...]
    logsig = jnp.minimum(x, 0.0) - jnp.log(1.0 + jnp.exp(-jnp.abs(x)))
    la = logsig * (LOG2E / GLA_GATE_NORMALIZER)
    la_hi = la.astype(BF16)
    la_mid = (la - la_hi.astype(F32)).astype(BF16)

    def cum(idx):
        mat = cum_ref[idx]
        return (jnp.dot(mat, la_hi, preferred_element_type=F32)
                + jnp.dot(mat, la_mid, preferred_element_type=F32))

    q = gq_ref[0].astype(F32)
    k = gk_ref[0].astype(F32)
    row = lax.broadcasted_iota(jnp.int32, (c, 1), 0)

    for lev in range(nlev):
        p = jnp.exp2(cum(lev))
        tside = ((row >> (nlev - 1 - lev)) & 1) == 1
        qt_ref[lev] = (q * jnp.where(tside, p, 0.0)).astype(BF16)
        kt_ref[lev] = (k * jnp.where(tside, 0.0, p)).astype(BF16)

    qb = (q * jnp.exp2(cum(nlev))).astype(BF16)
    kr = (k * jnp.exp2(cum(nlev + 1))).astype(BF16)
    qk = q * k

    ti = lax.broadcasted_iota(jnp.int32, (c, c), 0)
    si = lax.broadcasted_iota(jnp.int32, (c, c), 1)
    ones = jnp.ones((c, LANE), BF16)

    for hh in range(nh):
        ks = slice(hh * dk, (hh + 1) * dk)
        vs = slice(hh * dv, (hh + 1) * dv)
        a = lax.dot_general(qt_ref[0, :, ks], kt_ref[0, :, ks], _NT, preferred_element_type=F32)
        for lev in range(1, nlev):
            shift = nlev - lev
            same_group = (ti >> shift) == (si >> shift)
            al = lax.dot_general(qt_ref[lev, :, ks], kt_ref[lev, :, ks], _NT,
                                 preferred_element_type=F32)
            a = jnp.where(same_group, al, a)
        v = gv_ref[0, :, vs]
        s_old = s_ref[hh]
        diag = jnp.sum(qk[:, ks], axis=-1, keepdims=True)
        o = (jnp.dot(a.astype(BF16), v, preferred_element_type=F32)
             + jnp.dot(qb[:, ks], s_old.astype(BF16), preferred_element_type=F32)
             + diag * v.astype(F32))
        tot = (lax.dot_general(la_hi[:, ks], ones, _TN, preferred_element_type=F32)
               + lax.dot_general(la_mid[:, ks], ones, _TN, preferred_element_type=F32))
        dec = jnp.exp2(tot)
        dec = jnp.concatenate([dec] * (dv // LANE), axis=1)
        s_ref[hh] = dec * s_old + lax.dot_general(kr[:, ks], v, _TN, preferred_element_type=F32)
        on = _rms(o, gnorm_ref[...])
        g = gg_ref[0, :, vs].astype(F32)
        o_ref[0, :, vs] = (on * (g * jax.nn.sigmoid(g))).astype(BF16)


def _gla(proj, wup, bgk, gnorm, c=256):
    b, t, _ = proj.shape
    nlev = int(math.log2(c))
    kw = GLA_HEADS * GLA_HEAD_K
    vw = GLA_HEADS * GLA_HEAD_V
    q_off = 3 * DA_HEADS * DA_HEAD_V
    cum = _gla_cum_matrices(c)
    kern = functools.partial(_gla_kernel, c=c)
    return pl.pallas_call(
        kern,
        out_shape=jax.ShapeDtypeStruct((b, t, vw), BF16),
        grid=(b, t // c),
        in_specs=[
            pl.BlockSpec((1, c, kw), lambda bi, i: (bi, i, q_off // kw)),
            pl.BlockSpec((1, c, kw), lambda bi, i: (bi, i, q_off // kw + 1)),
            pl.BlockSpec((1, c, vw), lambda bi, i: (bi, i, (q_off + 2 * kw) // vw)),
            pl.BlockSpec((1, c, vw), lambda bi, i: (bi, i, (q_off + 2 * kw) // vw + 1)),
            pl.BlockSpec((1, c, LANE), lambda bi, i: (bi, i, (q_off + 2 * kw + 2 * vw) // LANE)),
            pl.BlockSpec((LANE, kw), lambda bi, i: (0, 0)),
            pl.BlockSpec((1, kw), lambda bi, i: (0, 0)),
            pl.BlockSpec((1, GLA_HEAD_V), lambda bi, i: (0, 0)),
            pl.BlockSpec((nlev + 2, c, c), lambda bi, i: (0, 0, 0)),
        ],
        out_specs=pl.BlockSpec((1, c, vw), lambda bi, i: (bi, i, 0)),
        scratch_shapes=[
            pltpu.VMEM((GLA_HEADS, GLA_HEAD_K, GLA_HEAD_V), F32),
            pltpu.VMEM((nlev, c, kw), BF16),
            pltpu.VMEM((nlev, c, kw), BF16),
        ],
        compiler_params=_cparams(("parallel", "arbitrary")),
        name="gla",
    )(proj, proj, proj, proj, proj, wup, bgk.reshape(1, kw), gnorm.reshape(1, GLA_HEAD_V), cum)


def _outproj_kernel(od_ref, og_ref, w1_ref, w2_ref, x_ref, mod_ref, g_ref, x1_ref, h2_ref):
    mix = (jnp.dot(od_ref[...], w1_ref[...], preferred_element_type=F32)
           + jnp.dot(og_ref[...], w2_ref[...], preferred_element_type=F32))
    x1 = x_ref[...] + mod_ref[0, 2:3, :] * mix
    x1_ref[...] = x1
    y = _rms(x1, g_ref[...])
    h2_ref[...] = (y * (1.0 + mod_ref[0, 4:5, :]) + mod_ref[0, 3:4, :]).astype(BF16)


def _out_projection(od, og, w_o, x2d, mod, g, seq, tm=512):
    n, d = x2d.shape
    half = od.shape[1]
    tiles_per_seq = seq // tm
    return pl.pallas_call(
        _outproj_kernel,
        out_shape=(jax.ShapeDtypeStruct((n, d), F32), jax.ShapeDtypeStruct((n, d), BF16)),
        grid=(n // tm,),
        in_specs=[
            pl.BlockSpec((tm, half), lambda i: (i, 0)),
            pl.BlockSpec((tm, half), lambda i: (i, 0)),
            pl.BlockSpec((half, d), lambda i: (0, 0)),
            pl.BlockSpec((half, d), lambda i: (1, 0)),
            pl.BlockSpec((tm, d), lambda i: (i, 0)),
            pl.BlockSpec((1, N_MOD, d), lambda i: (i // tiles_per_seq, 0, 0)),
            pl.BlockSpec((1, d), lambda i: (0, 0)),
        ],
        out_specs=(pl.BlockSpec((tm, d), lambda i: (i, 0)), pl.BlockSpec((tm, d), lambda i: (i, 0))),
        compiler_params=_cparams(("parallel",)),
        name="out_proj_residual_norm",
    )(od, og, w_o, w_o, x2d, mod, g.reshape(1, d))


def _ffn_kernel(h2_ref, wu_ref, wg_ref, wc_ref, bc_ref, wd_ref, x1_ref, mod_ref, gfin_ref,
                o_ref, acc_ref, carry_ref, *, tiles_per_seq, final):
    i = pl.program_id(0)
    j = pl.program_id(1)
    tm = h2_ref.shape[0]
    tf = wu_ref.shape[1]
    sub = 8

    @pl.when(j == 0)
    def _():
        acc_ref[...] = jnp.zeros_like(acc_ref)

    h2 = h2_ref[...]
    g = jnp.dot(h2, wg_ref[...], preferred_element_type=F32)
    u = jnp.dot(h2, wu_ref[...], preferred_element_type=F32)
    g3 = g.reshape(tm // sub, sub, tf)

    prev = carry_ref[j]
    prev = jnp.where(i % tiles_per_seq == 0, jnp.zeros_like(prev), prev)
    carry_ref[j] = g3[tm // sub - 1]
    sl = lax.broadcasted_iota(jnp.int32, (1, sub, tf), 1)

    def shifted(n):
        rot = pltpu.roll(g3, n, 1)
        before = jnp.concatenate([pltpu.roll(prev[None], n, 1), rot[:-1]], axis=0)
        return jnp.where(sl < n, before, rot)

    wc = wc_ref[...] * SQRT_HALF
    z = bc_ref[...] * SQRT_HALF + shifted(2) * wc[0:1] + shifted(1) * wc[1:2] + g3 * wc[2:3]
    act = (z * (1.0 + lax.erf(z))).reshape(tm, tf) * u
    acc_ref[...] += jnp.dot(act.astype(BF16), wd_ref[...], preferred_element_type=F32)

    @pl.when(j == pl.num_programs(1) - 1)
    def _():
        x2 = x1_ref[...] + (mod_ref[0, 5:6, :] * SQRT_HALF) * acc_ref[...]
        if final:
            x2 = _rms(x2, gfin_ref[...])
        o_ref[...] = x2


def _ffn(h2, w_up, w_conv, b_conv, w_down, x1, mod, g_final, seq, final, tm=512, tf=512):
    n, d = x1.shape
    f = w_down.shape[0]
    nf = f // tf
    tiles_per_seq = seq // tm
    kern = functools.partial(_ffn_kernel, tiles_per_seq=tiles_per_seq, final=final)
    return pl.pallas_call(
        kern,
        out_shape=jax.ShapeDtypeStruct((n, d), F32),
        grid=(n // tm, nf),
        in_specs=[
            pl.BlockSpec((tm, d), lambda i, j: (i, 0)),
            pl.BlockSpec((d, tf), lambda i, j: (0, j)),
            pl.BlockSpec((d, tf), lambda i, j: (0, nf + j)),
            pl.BlockSpec((CONV_WIDTH, tf), lambda i, j: (0, j)),
            pl.BlockSpec((1, tf), lambda i, j: (0, j)),
            pl.BlockSpec((tf, d), lambda i, j: (j, 0)),
            pl.BlockSpec((tm, d), lambda i, j: (i, 0)),
            pl.BlockSpec((1, N_MOD, d), lambda i, j: (i // tiles_per_seq, 0, 0)),
            pl.BlockSpec((1, d), lambda i, j: (0, 0)),
        ],
        out_specs=pl.BlockSpec((tm, d), lambda i, j: (i, 0)),
        scratch_shapes=[
            pltpu.VMEM((tm, d), F32),
            pltpu.VMEM((nf, 8, tf), F32),
        ],
        compiler_params=_cparams(("arbitrary", "arbitrary")),
        name="conv_ffn_residual",
    )(h2, w_up, w_up, w_conv, b_conv.reshape(1, f), w_down, x1, mod, g_final.reshape(1, d))


def kernel(x, c, w_ada, b_ada, g_mix_norm, w_in, diff_lambda, g_diff_subln, w_gk_up, b_gk,
           g_gla_norm, w_o, g_ffn_norm, w_up, w_conv, b_conv, w_down, g_final):
    b, t, d = x.shape
    depth = w_ada.shape[0]
    n_proj = w_in.shape[2]
    in_tile = 1280
    n_proj_pad = -(-n_proj // in_tile) * in_tile
    col_scale = np.ones((1, n_proj_pad), np.float32)
    da_w = DA_HEADS * DA_HEAD_V
    col_scale[0, :da_w] = DA_HEAD_QK ** -0.5 * LOG2E
    col_scale[0, 3 * da_w:3 * da_w + GLA_HEADS * GLA_HEAD_K] = GLA_HEAD_K ** -0.5
    col_scale = jnp.asarray(col_scale)

    mod_all = _modulation(c, w_ada, b_ada).reshape(depth, b, N_MOD, d)
    x2d = x.reshape(b * t, d)
    for l in range(depth):
        mod = mod_all[l]
        w_in_l = jnp.pad(w_in[l].astype(BF16), ((0, 0), (0, n_proj_pad - n_proj)))
        wup_l = jnp.pad(w_gk_up[l].astype(BF16), ((0, LANE - GLA_GATE_RANK), (0, 0)))

        proj = _in_projection(x2d, mod, g_mix_norm[l], w_in_l, col_scale, t, tn=in_tile)
        proj = proj.reshape(b, t, n_proj_pad)
        od = _diff_attention(proj, diff_lambda[l], g_diff_subln[l], l)
        og = _gla(proj, wup_l, b_gk[l], g_gla_norm[l])
        x1, h2 = _out_projection(od.reshape(b * t, -1), og.reshape(b * t, -1), w_o[l].astype(BF16),
                                 x2d, mod, g_ffn_norm[l], t)
        x2d = _ffn(h2, w_up[l].astype(BF16), w_conv[l], b_conv[l], w_down[l].astype(BF16), x1, mod,
                   g_final, t, final=(l == depth - 1))
    return x2d.reshape(b, t, d)
```

```python
import functools
import math

import numpy as np
import jax
import jax.numpy as jnp
from jax import lax
from jax.experimental import pallas as pl
from jax.experimental.pallas import tpu as pltpu

BF16 = jnp.bfloat16
F32 = jnp.float32

NORM_EPS = 1e-6
N_MOD = 6
DA_HEADS = 8
DA_HEAD_V = 128
DA_HEAD_QK = 64
GLA_HEADS = 4
GLA_HEAD_K = 128
GLA_HEAD_V = 256
GLA_GATE_RANK = 16
GLA_GATE_NORMALIZER = 16.0
CONV_WIDTH = 3

LANE = 128
V7X_VMEM_LIMIT = 56 * 1024 * 1024
LOG2E = 1.4426950408889634
SQRT_HALF = 0.7071067811865476

_NT = (((1,), (1,)), ((), ()))
_TN = (((0,), (0,)), ((), ()))


def _cparams(sem):
    return pltpu.CompilerParams(dimension_semantics=sem, vmem_limit_bytes=V7X_VMEM_LIMIT)


def _lambda_init(layer_idx):
    return 0.8 - 0.6 * math.exp(-0.3 * layer_idx)


def _rms(x, g):
    ms = jnp.mean(x * x, axis=-1, keepdims=True)
    return x * lax.rsqrt(ms + NORM_EPS) * g


def _mod_kernel(c_ref, w_ref, b_ref, o_ref):
    c = c_ref[...]
    ca = (c * jax.nn.sigmoid(c)).astype(BF16)
    w = w_ref[0].astype(BF16)
    o_ref[0] = jnp.dot(ca, w, preferred_element_type=F32) + b_ref[0]


def _modulation(c, w_ada, b_ada, tn=1024):
    depth, d, n = w_ada.shape
    b = c.shape[0]
    return pl.pallas_call(
        _mod_kernel,
        out_shape=jax.ShapeDtypeStruct((depth, b, n), F32),
        grid=(depth, n // tn),
        in_specs=[
            pl.BlockSpec((b, d), lambda l, j: (0, 0)),
            pl.BlockSpec((1, d, tn), lambda l, j: (l, 0, j)),
            pl.BlockSpec((1, 1, tn), lambda l, j: (l, 0, j)),
        ],
        out_specs=pl.BlockSpec((1, b, tn), lambda l, j: (l, 0, j)),
        compiler_params=_cparams(("parallel", "parallel")),
        name="adaln_mod",
    )(c, w_ada, b_ada.reshape(depth, 1, n))


def _inproj_kernel(x_ref, mod_ref, g_ref, w_ref, cs_ref, o_ref, h_ref, *, nchunk):
    i = pl.program_id(0)
    j = pl.program_id(1)
    cr = x_ref.shape[0] // nchunk

    def norm_chunk():
        c = jnp.minimum(j, nchunk - 1)
        rows = pl.ds(pl.multiple_of(c * cr, cr), cr)
        y = _rms(x_ref[rows, :], g_ref[...])
        h_ref[i % 2, rows, :] = (y * (1.0 + mod_ref[0, 1:2, :]) + mod_ref[0, 0:1, :]).astype(BF16)

    @pl.when(i == 0)
    def _():
        norm_chunk()

    @pl.when(i > 0)
    def _():
        acc = jnp.dot(h_ref[(i + 1) % 2], w_ref[...], preferred_element_type=F32)
        o_ref[...] = (acc * cs_ref[...]).astype(BF16)
        norm_chunk()


def _in_projection(x2d, mod, g, w, layer, col_scale, seq, tm=1024, tn=1280):
    n, d = x2d.shape
    p = w.shape[1]
    tm = min(tm, seq)
    tiles_per_seq = seq // tm
    nt = n // tm
    nj = p // tn
    nchunk = max(c for c in range(1, nj + 1) if tm % (16 * c) == 0)
    col = lambda i, j: jnp.where(i == 0, 0, j)
    return pl.pallas_call(
        functools.partial(_inproj_kernel, nchunk=nchunk),
        out_shape=jax.ShapeDtypeStruct((n, p), BF16),
        grid=(nt + 1, nj),
        in_specs=[
            pl.BlockSpec((tm, d), lambda i, j: (jnp.minimum(i, nt - 1), 0)),
            pl.BlockSpec((1, N_MOD, d), lambda i, j: (jnp.minimum(i, nt - 1) // tiles_per_seq, 0, 0)),
            pl.BlockSpec((1, d), lambda i, j: (0, 0)),
            pl.BlockSpec((d, tn), lambda i, j: (layer, col(i, j))),
            pl.BlockSpec((1, tn), lambda i, j: (0, col(i, j))),
        ],
        out_specs=pl.BlockSpec((tm, tn), lambda i, j: (jnp.maximum(i - 1, 0), col(i, j))),
        scratch_shapes=[pltpu.VMEM((2, tm, d), BF16)],
        compiler_params=_cparams(("arbitrary", "arbitrary")),
        name="norm_in_proj",
    )(x2d, mod, g.reshape(1, d), w, col_scale)


def _alibi_slopes(n):
    start = 2.0 ** (-8.0 / n)
    return np.array([start ** (i + 1) for i in range(n)], dtype=np.float32)


def _bf16_split(x):
    hi = x.astype(BF16).astype(np.float32)
    lo = (x - hi).astype(BF16).astype(np.float32)
    return hi, lo


def _alibi_aug(t, blk):
    j = np.arange(t)
    kaug = np.zeros((t, LANE), np.float32)
    kaug[:, 0] = kaug[:, 2] = j % 256
    kaug[:, 1] = kaug[:, 3] = j // 256
    kaug[:, 4] = 1.0
    c = _alibi_slopes(DA_HEADS).astype(np.float64) * LOG2E
    c_hi, c_lo = _bf16_split(c.astype(np.float32))
    qaug = np.zeros((DA_HEADS, t // blk, LANE), np.float32)
    qaug[:, :, 0] = c_hi[:, None]
    qaug[:, :, 1] = 256.0 * c_hi[:, None]
    qaug[:, :, 2] = c_lo[:, None]
    qaug[:, :, 3] = 256.0 * c_lo[:, None]
    qaug[:, :, 4] = -(c[:, None] * (np.arange(t // blk)[None, :] * blk))
    return jnp.asarray(kaug.astype(BF16)), jnp.asarray(qaug.astype(BF16))


def _dattn_kernel(q_ref, k_ref, v_ref, kaug_ref, qaug_ref, lamp_ref, gsub_ref, o_ref, *, blk, lam_init):
    t = q_ref.shape[1]
    nq = t // blk
    heads = q_ref.shape[2] // LANE
    kaug = kaug_ref[...]
    ones = jnp.ones((t, LANE), BF16)

    lamp = lamp_ref[...]
    s1 = jnp.sum(lamp[0:1] * lamp[1:2], axis=-1, keepdims=True)
    s2 = jnp.sum(lamp[2:3] * lamp[3:4], axis=-1, keepdims=True)
    lam = jnp.exp(s1) - jnp.exp(s2) + lam_init

    lane = lax.broadcasted_iota(jnp.int32, (blk, LANE), 1)
    ri = lax.broadcasted_iota(jnp.int32, (2 * blk, blk), 0)
    ci = lax.broadcasted_iota(jnp.int32, (2 * blk, blk), 1)
    causal = (ri & (blk - 1)) >= ci
    zero = jnp.zeros((blk, LANE), BF16)

    def scores(h, i):
        hs = slice(h * LANE, (h + 1) * LANE)
        qi = q_ref[0, i * blk:(i + 1) * blk, hs]
        aug = jnp.broadcast_to(qaug_ref[h, i:i + 1, :], (blk, LANE))
        q1 = jnp.concatenate([jnp.where(lane < DA_HEAD_QK, qi, zero), aug], axis=1)
        q2 = jnp.concatenate([jnp.where(lane >= DA_HEAD_QK, qi, zero), aug], axis=1)
        qc = jnp.concatenate([q1, q2], axis=0)
        lo = i * blk
        kd = jnp.concatenate([k_ref[0, lo:lo + blk, hs], kaug[lo:lo + blk]], axis=1)
        sd = lax.dot_general(qc, kd, _NT, preferred_element_type=F32)
        sm = None
        if i > 0:
            km = jnp.concatenate([k_ref[0, :lo, hs], kaug[:lo]], axis=1)
            sm = lax.dot_general(qc, km, _NT, preferred_element_type=F32)
        return sd, sm

    def finish(h, i, sd, sm):
        hs = slice(h * LANE, (h + 1) * LANE)
        lo = i * blk
        sd = jnp.where(causal, sd, -jnp.inf)
        m = jnp.max(sd, axis=-1, keepdims=True)
        if i > 0:
            m = jnp.maximum(m, jnp.max(sm, axis=-1, keepdims=True))
        vd = jnp.concatenate([v_ref[0, lo:lo + blk, hs], ones[:blk]], axis=1)
        o = jnp.dot(jnp.exp2(sd - m).astype(BF16), vd, preferred_element_type=F32)
        if i > 0:
            vm = jnp.concatenate([v_ref[0, :lo, hs], ones[:lo]], axis=1)
            o = o + jnp.dot(jnp.exp2(sm - m).astype(BF16), vm, preferred_element_type=F32)
        on = o[:, :DA_HEAD_V] * (1.0 / o[:, DA_HEAD_V:DA_HEAD_V + 1])
        od = on[:blk] - lam * on[blk:]
        od = _rms(od, gsub_ref[...]) * (1.0 - lam_init)
        o_ref[0, lo:lo + blk, hs] = od.astype(BF16)

    work = [(h, i) for h in range(heads) for i in range(nq)]
    pending = scores(*work[0])
    for n, hi in enumerate(work):
        nxt = scores(*work[n + 1]) if n + 1 < len(work) else None
        finish(*hi, *pending)
        pending = nxt


def _diff_attention(proj, lamp, gsub, layer_idx, blk=256, heads_per_step=2):
    b, t, _ = proj.shape
    kaug, qaug = _alibi_aug(t, blk)
    hps = heads_per_step
    w = hps * LANE
    ngrp = DA_HEADS // hps
    kern = functools.partial(_dattn_kernel, blk=blk, lam_init=_lambda_init(layer_idx))
    return pl.pallas_call(
        kern,
        out_shape=jax.ShapeDtypeStruct((b, t, DA_HEADS * DA_HEAD_V), BF16),
        grid=(b, ngrp),
        in_specs=[
            pl.BlockSpec((1, t, w), lambda bi, h: (bi, 0, h)),
            pl.BlockSpec((1, t, w), lambda bi, h: (bi, 0, ngrp + h)),
            pl.BlockSpec((1, t, w), lambda bi, h: (bi, 0, 2 * ngrp + h)),
            pl.BlockSpec((t, LANE), lambda bi, h: (0, 0)),
            pl.BlockSpec((hps, t // blk, LANE), lambda bi, h: (h, 0, 0)),
            pl.BlockSpec((4, DA_HEAD_QK), lambda bi, h: (0, 0)),
            pl.BlockSpec((1, DA_HEAD_V), lambda bi, h: (0, 0)),
        ],
        out_specs=pl.BlockSpec((1, t, w), lambda bi, h: (bi, 0, h)),
        compiler_params=_cparams(("parallel", "parallel")),
        name="diff_attention",
    )(proj, proj, proj, kaug, qaug, lamp, gsub.reshape(1, DA_HEAD_V))


def _gla_cum_matrices(c):
    nlev = int(math.log2(c))
    m = np.zeros((nlev + 2, c, c), np.float32)
    for lev in range(nlev):
        h = c >> (lev + 1)
        for r in range(c):
            mid = (r // (2 * h)) * 2 * h + h
            if r % (2 * h) >= h:
                m[lev, r, mid:r + 1] = 1.0
            else:
                m[lev, r, r + 1:mid] = 1.0
    for r in range(c):
        m[nlev, r, :r + 1] = 1.0
        m[nlev + 1, r, r + 1:] = 1.0
    return jnp.asarray(m.astype(BF16))


def _gla_kernel(gq_ref, gk_ref, gv_ref, gg_ref, glow_ref, wup_ref, bgk_ref, gnorm_ref, cum_ref,
                o_ref, s_ref, qt_ref, kt_ref, *, c):
    nlev = int(math.log2(c))
    dk, dv, nh = GLA_HEAD_K, GLA_HEAD_V, GLA_HEADS

    @pl.when(pl.program_id(1) == 0)
    def _():
        s_ref[...] = jnp.zeros_like(s_ref)

    x = jnp.dot(glow_ref[0], wup_ref[...], preferred_element_type=F32) + bgk_ref[...]
    logsig = jnp.minimum(x, 0.0) - jnp.log(1.0 + jnp.exp(-jnp.abs(x)))
    la = logsig * (LOG2E / GLA_GATE_NORMALIZER)
    la_hi = la.astype(BF16)
    la_mid = (la - la_hi.astype(F32)).astype(BF16)

    def cum(idx):
        mat = cum_ref[idx]
        return (jnp.dot(mat, la_hi, preferred_element_type=F32)
                + jnp.dot(mat, la_mid, preferred_element_type=F32))

    q = gq_ref[0].astype(F32)
    k = gk_ref[0].astype(F32)
    row = lax.broadcasted_iota(jnp.int32, (c, 1), 0)

    for lev in range(nlev):
        p = jnp.exp2(cum(lev))
        tside = ((row >> (nlev - 1 - lev)) & 1) == 1
        qt_ref[lev] = (q * jnp.where(tside, p, 0.0)).astype(BF16)
        kt_ref[lev] = (k * jnp.where(tside, 0.0, p)).astype(BF16)

    qb = (q * jnp.exp2(cum(nlev))).astype(BF16)
    kr = (k * jnp.exp2(cum(nlev + 1))).astype(BF16)
    qk = q * k

    ti = lax.broadcasted_iota(jnp.int32, (c, c), 0)
    si = lax.broadcasted_iota(jnp.int32, (c, c), 1)
    ones = jnp.ones((c, LANE), BF16)

    for hh in range(nh):
        ks = slice(hh * dk, (hh + 1) * dk)
        vs = slice(hh * dv, (hh + 1) * dv)
        a = lax.dot_general(qt_ref[0, :, ks], kt_ref[0, :, ks], _NT, preferred_element_type=F32)
        for lev in range(1, nlev):
            shift = nlev - lev
            same_group = (ti >> shift) == (si >> shift)
            al = lax.dot_general(qt_ref[lev, :, ks], kt_ref[lev, :, ks], _NT,
                                 preferred_element_type=F32)
            a = jnp.where(same_group, al, a)
        v = gv_ref[0, :, vs]
        s_old = s_ref[hh]
        diag = jnp.sum(qk[:, ks], axis=-1, keepdims=True)
        o = (jnp.dot(a.astype(BF16), v, preferred_element_type=F32)
             + jnp.dot(qb[:, ks], s_old.astype(BF16), preferred_element_type=F32)
             + diag * v.astype(F32))
        tot = (lax.dot_general(la_hi[:, ks], ones, _TN, preferred_element_type=F32)
               + lax.dot_general(la_mid[:, ks], ones, _TN, preferred_element_type=F32))
        dec = jnp.exp2(tot)
        dec = jnp.concatenate([dec] * (dv // LANE), axis=1)
        s_ref[hh] = dec * s_old + lax.dot_general(kr[:, ks], v, _TN, preferred_element_type=F32)
        on = _rms(o, gnorm_ref[...])
        g = gg_ref[0, :, vs].astype(F32)
        o_ref[0, :, vs] = (on * (g * jax.nn.sigmoid(g))).astype(BF16)


def _gla(proj, wup, bgk, gnorm, c=256):
    b, t, _ = proj.shape
    nlev = int(math.log2(c))
    kw = GLA_HEADS * GLA_HEAD_K
    vw = GLA_HEADS * GLA_HEAD_V
    q_off = 3 * DA_HEADS * DA_HEAD_V
    cum = _gla_cum_matrices(c)
    kern = functools.partial(_gla_kernel, c=c)
    return pl.pallas_call(
        kern,
        out_shape=jax.ShapeDtypeStruct((b, t, vw), BF16),
        grid=(b, t // c),
        in_specs=[
            pl.BlockSpec((1, c, kw), lambda bi, i: (bi, i, q_off // kw)),
            pl.BlockSpec((1, c, kw), lambda bi, i: (bi, i, q_off // kw + 1)),
            pl.BlockSpec((1, c, vw), lambda bi, i: (bi, i, (q_off + 2 * kw) // vw)),
            pl.BlockSpec((1, c, vw), lambda bi, i: (bi, i, (q_off + 2 * kw) // vw + 1)),
            pl.BlockSpec((1, c, LANE), lambda bi, i: (bi, i, (q_off + 2 * kw + 2 * vw) // LANE)),
            pl.BlockSpec((LANE, kw), lambda bi, i: (0, 0)),
            pl.BlockSpec((1, kw), lambda bi, i: (0, 0)),
            pl.BlockSpec((1, GLA_HEAD_V), lambda bi, i: (0, 0)),
            pl.BlockSpec((nlev + 2, c, c), lambda bi, i: (0, 0, 0)),
        ],
        out_specs=pl.BlockSpec((1, c, vw), lambda bi, i: (bi, i, 0)),
        scratch_shapes=[
            pltpu.VMEM((GLA_HEADS, GLA_HEAD_K, GLA_HEAD_V), F32),
            pltpu.VMEM((nlev, c, kw), BF16),
            pltpu.VMEM((nlev, c, kw), BF16),
        ],
        compiler_params=_cparams(("parallel", "arbitrary")),
        name="gla",
    )(proj, proj, proj, proj, proj, wup, bgk.reshape(1, kw), gnorm.reshape(1, GLA_HEAD_V), cum)


def _outproj_kernel(od_ref, og_ref, w1_ref, w2_ref, x_ref, mod_ref, g_ref, x1_ref, h2_ref):
    mix = (jnp.dot(od_ref[...], w1_ref[...], preferred_element_type=F32)
           + jnp.dot(og_ref[...], w2_ref[...], preferred_element_type=F32))
    x1 = x_ref[...] + mod_ref[0, 2:3, :] * mix
    x1_ref[...] = x1
    y = _rms(x1, g_ref[...])
    h2_ref[...] = (y * (1.0 + mod_ref[0, 4:5, :]) + mod_ref[0, 3:4, :]).astype(BF16)


def _out_projection(od, og, w_o, layer, x2d, mod, g, seq, tm=512):
    n, d = x2d.shape
    half = od.shape[1]
    tiles_per_seq = seq // tm
    return pl.pallas_call(
        _outproj_kernel,
        out_shape=(jax.ShapeDtypeStruct((n, d), F32), jax.ShapeDtypeStruct((n, d), BF16)),
        grid=(n // tm,),
        in_specs=[
            pl.BlockSpec((tm, half), lambda i: (i, 0)),
            pl.BlockSpec((tm, half), lambda i: (i, 0)),
            pl.BlockSpec((half, d), lambda i: (2 * layer, 0)),
            pl.BlockSpec((half, d), lambda i: (2 * layer + 1, 0)),
            pl.BlockSpec((tm, d), lambda i: (i, 0)),
            pl.BlockSpec((1, N_MOD, d), lambda i: (i // tiles_per_seq, 0, 0)),
            pl.BlockSpec((1, d), lambda i: (0, 0)),
        ],
        out_specs=(pl.BlockSpec((tm, d), lambda i: (i, 0)), pl.BlockSpec((tm, d), lambda i: (i, 0))),
        compiler_params=_cparams(("parallel",)),
        name="out_proj_residual_norm",
    )(od, og, w_o, w_o, x2d, mod, g.reshape(1, d))


def _ffn_kernel(h2_ref, wu_ref, wg_ref, wc_ref, bc_ref, wd_ref, x1_hbm, mod_ref, gfin_ref,
                o_ref, x1_buf, x1_sem, carry_ref, *, tiles_per_seq, final):
    i = pl.program_id(0)
    j = pl.program_id(1)
    last = pl.num_programs(1) - 1
    tm = h2_ref.shape[0]
    tf = wu_ref.shape[1]
    sub = 8

    def x1_copy():
        return pltpu.make_async_copy(x1_hbm.at[pl.ds(pl.multiple_of(i * tm, tm), tm), :], x1_buf, x1_sem)

    @pl.when(j == 0)
    def _():
        x1_copy().start()

    def down_partial():
        h2 = h2_ref[...]
        g = jnp.dot(h2, wg_ref[...], preferred_element_type=F32)
        u = jnp.dot(h2, wu_ref[...], preferred_element_type=F32)
        g3 = g.reshape(tm // sub, sub, tf)

        prev = carry_ref[j]
        prev = jnp.where(i % tiles_per_seq == 0, jnp.zeros_like(prev), prev)
        carry_ref[j] = g3[tm // sub - 1]
        sl = lax.broadcasted_iota(jnp.int32, (1, sub, tf), 1)

        def shifted(n):
            rot = pltpu.roll(g3, n, 1)
            before = jnp.concatenate([pltpu.roll(prev[None], n, 1), rot[:-1]], axis=0)
            return jnp.where(sl < n, before, rot)

        wc = wc_ref[...] * SQRT_HALF
        z = bc_ref[...] * SQRT_HALF + shifted(2) * wc[0:1] + shifted(1) * wc[1:2] + g3 * wc[2:3]
        act = (z * (1.0 + lax.erf(z))).reshape(tm, tf) * u
        return jnp.dot(act.astype(BF16), wd_ref[...], preferred_element_type=F32)

    @pl.when(j == 0)
    def _():
        o_ref[...] = down_partial()

    @pl.when(j > 0)
    def _():
        o_ref[...] += down_partial()

    @pl.when(j == last)
    def _():
        x1_copy().wait()
        x2 = x1_buf[...] + (mod_ref[0, 5:6, :] * SQRT_HALF) * o_ref[...]
        if final:
            x2 = _rms(x2, gfin_ref[...])
        o_ref[...] = x2


def _ffn(h2, w_up, w_conv, b_conv, w_down, layer, x1, mod, g_final, seq, final, tm=1024, tf=512):
    n, d = x1.shape
    f = w_conv.shape[1]
    nf = f // tf
    tm = min(tm, seq)
    tiles_per_seq = seq // tm
    kern = functools.partial(_ffn_kernel, tiles_per_seq=tiles_per_seq, final=final)
    return pl.pallas_call(
        kern,
        out_shape=jax.ShapeDtypeStruct((n, d), F32),
        grid=(n // tm, nf),
        in_specs=[
            pl.BlockSpec((tm, d), lambda i, j: (i, 0)),
            pl.BlockSpec((d, tf), lambda i, j: (layer, j)),
            pl.BlockSpec((d, tf), lambda i, j: (layer, nf + j)),
            pl.BlockSpec((CONV_WIDTH, tf), lambda i, j: (0, j)),
            pl.BlockSpec((1, tf), lambda i, j: (0, j)),
            pl.BlockSpec((tf, d), lambda i, j: (layer * nf + j, 0)),
            pl.BlockSpec(memory_space=pl.ANY),
            pl.BlockSpec((1, N_MOD, d), lambda i, j: (i // tiles_per_seq, 0, 0)),
            pl.BlockSpec((1, d), lambda i, j: (0, 0)),
        ],
        out_specs=pl.BlockSpec((tm, d), lambda i, j: (i, 0)),
        scratch_shapes=[
            pltpu.VMEM((tm, d), F32),
            pltpu.SemaphoreType.DMA,
            pltpu.VMEM((nf, 8, tf), F32),
        ],
        compiler_params=_cparams(("arbitrary", "arbitrary")),
        name="conv_ffn_residual",
    )(h2, w_up, w_up, w_conv, b_conv.reshape(1, f), w_down, x1, mod, g_final.reshape(1, d))


def kernel(x, c, w_ada, b_ada, g_mix_norm, w_in, diff_lambda, g_diff_subln, w_gk_up, b_gk,
           g_gla_norm, w_o, g_ffn_norm, w_up, w_conv, b_conv, w_down, g_final):
    b, t, d = x.shape
    depth = w_ada.shape[0]
    n_proj = w_in.shape[2]
    in_tile = 1280
    n_proj_pad = -(-n_proj // in_tile) * in_tile
    col_scale = np.ones((1, n_proj_pad), np.float32)
    da_w = DA_HEADS * DA_HEAD_V
    col_scale[0, :da_w] = DA_HEAD_QK ** -0.5 * LOG2E
    col_scale[0, 3 * da_w:3 * da_w + GLA_HEADS * GLA_HEAD_K] = GLA_HEAD_K ** -0.5
    col_scale = jnp.asarray(col_scale)

    w_in_b = jnp.pad(w_in.astype(BF16), ((0, 0), (0, 0), (0, n_proj_pad - n_proj))).reshape(depth * d, -1)
    w_o_b = w_o.astype(BF16).reshape(-1, d)
    w_up_b = w_up.astype(BF16).reshape(depth * d, -1)
    w_down_b = w_down.astype(BF16).reshape(-1, d)

    mod_all = _modulation(c, w_ada, b_ada).reshape(depth, b, N_MOD, d)
    x2d = x.reshape(b * t, d)
    for l in range(depth):
        mod = mod_all[l]
        wup_l = jnp.pad(w_gk_up[l].astype(BF16), ((0, LANE - GLA_GATE_RANK), (0, 0)))

        proj = _in_projection(x2d, mod, g_mix_norm[l], w_in_b, l, col_scale, t, tn=in_tile)
        proj = proj.reshape(b, t, n_proj_pad)
        od = _diff_attention(proj, diff_lambda[l], g_diff_subln[l], l)
        og = _gla(proj, wup_l, b_gk[l], g_gla_norm[l])
        x1, h2 = _out_projection(od.reshape(b * t, -1), og.reshape(b * t, -1), w_o_b, l,
                                 x2d, mod, g_ffn_norm[l], t)
        x2d = _ffn(h2, w_up_b, w_conv[l], b_conv[l], w_down_b, l, x1, mod,
                   g_final, t, final=(l == depth - 1))
    return x2d.reshape(b, t, d)
```

```python
import functools
import math

import numpy as np
import jax
import jax.numpy as jnp
from jax import lax
from jax.experimental import pallas as pl
from jax.experimental.pallas import tpu as pltpu

BF16 = jnp.bfloat16
F32 = jnp.float32

NORM_EPS = 1e-6
N_MOD = 6
DA_HEADS = 8
DA_HEAD_V = 128
DA_HEAD_QK = 64
GLA_HEADS = 4
GLA_HEAD_K = 128
GLA_HEAD_V = 256
GLA_GATE_RANK = 16
GLA_GATE_NORMALIZER = 16.0
CONV_WIDTH = 3

LANE = 128
V7X_VMEM_LIMIT = 56 * 1024 * 1024
LOG2E = 1.4426950408889634
SQRT_HALF = 0.7071067811865476

_NT = (((1,), (1,)), ((), ()))
_TN = (((0,), (0,)), ((), ()))


def _cparams(sem):
    return pltpu.CompilerParams(dimension_semantics=sem, vmem_limit_bytes=V7X_VMEM_LIMIT)


def _lambda_init(layer_idx):
    return 0.8 - 0.6 * math.exp(-0.3 * layer_idx)


def _rms(x, g):
    ms = jnp.mean(x * x, axis=-1, keepdims=True)
    return x * lax.rsqrt(ms + NORM_EPS) * g


def _mod_kernel(c_ref, w_ref, b_ref, o_ref):
    c = c_ref[...]
    ca = (c * jax.nn.sigmoid(c)).astype(BF16)
    w = w_ref[0].astype(BF16)
    o_ref[0] = jnp.dot(ca, w, preferred_element_type=F32) + b_ref[0]


def _modulation(c, w_ada, b_ada, tn=1024):
    depth, d, n = w_ada.shape
    b = c.shape[0]
    return pl.pallas_call(
        _mod_kernel,
        out_shape=jax.ShapeDtypeStruct((depth, b, n), F32),
        grid=(depth, n // tn),
        in_specs=[
            pl.BlockSpec((b, d), lambda l, j: (0, 0)),
            pl.BlockSpec((1, d, tn), lambda l, j: (l, 0, j)),
            pl.BlockSpec((1, 1, tn), lambda l, j: (l, 0, j)),
        ],
        out_specs=pl.BlockSpec((1, b, tn), lambda l, j: (l, 0, j)),
        compiler_params=_cparams(("parallel", "parallel")),
        name="adaln_mod",
    )(c, w_ada, b_ada.reshape(depth, 1, n))


def _inproj_kernel(x_ref, mod_ref, g_ref, w_ref, cs_ref, o_ref, h_ref, *, nchunk):
    i = pl.program_id(0)
    j = pl.program_id(1)
    cr = x_ref.shape[0] // nchunk

    def norm_chunk():
        c = jnp.minimum(j, nchunk - 1)
        rows = pl.ds(pl.multiple_of(c * cr, cr), cr)
        y = _rms(x_ref[rows, :], g_ref[...])
        h_ref[i % 2, rows, :] = (y * (1.0 + mod_ref[0, 1:2, :]) + mod_ref[0, 0:1, :]).astype(BF16)

    @pl.when(i == 0)
    def _():
        norm_chunk()

    @pl.when(i > 0)
    def _():
        acc = jnp.dot(h_ref[(i + 1) % 2], w_ref[...], preferred_element_type=F32)
        o_ref[...] = (acc * cs_ref[...]).astype(BF16)
        norm_chunk()


def _in_projection(x2d, mod, g, w, layer, col_scale, seq, tm=1024, tn=1280):
    n, d = x2d.shape
    p = w.shape[1]
    tm = min(tm, seq)
    tiles_per_seq = seq // tm
    nt = n // tm
    nj = p // tn
    nchunk = max(c for c in range(1, nj + 1) if tm % (16 * c) == 0)
    col = lambda i, j: jnp.where(i == 0, 0, j)
    return pl.pallas_call(
        functools.partial(_inproj_kernel, nchunk=nchunk),
        out_shape=jax.ShapeDtypeStruct((n, p), BF16),
        grid=(nt + 1, nj),
        in_specs=[
            pl.BlockSpec((tm, d), lambda i, j: (jnp.minimum(i, nt - 1), 0)),
            pl.BlockSpec((1, N_MOD, d), lambda i, j: (jnp.minimum(i, nt - 1) // tiles_per_seq, 0, 0)),
            pl.BlockSpec((1, d), lambda i, j: (0, 0)),
            pl.BlockSpec((d, tn), lambda i, j: (layer, col(i, j))),
            pl.BlockSpec((1, tn), lambda i, j: (0, col(i, j))),
        ],
        out_specs=pl.BlockSpec((tm, tn), lambda i, j: (jnp.maximum(i - 1, 0), col(i, j))),
        scratch_shapes=[pltpu.VMEM((2, tm, d), BF16)],
        compiler_params=_cparams(("arbitrary", "arbitrary")),
        name="norm_in_proj",
    )(x2d, mod, g.reshape(1, d), w, col_scale)


def _alibi_slopes(n):
    start = 2.0 ** (-8.0 / n)
    return np.array([start ** (i + 1) for i in range(n)], dtype=np.float32)


def _bf16_split(x):
    hi = x.astype(BF16).astype(np.float32)
    lo = (x - hi).astype(BF16).astype(np.float32)
    return hi, lo


def _alibi_aug(t, blk):
    j = np.arange(t)
    kaug = np.zeros((t, LANE), np.float32)
    kaug[:, 0] = kaug[:, 2] = j % 256
    kaug[:, 1] = kaug[:, 3] = j // 256
    kaug[:, 4] = 1.0
    c = _alibi_slopes(DA_HEADS).astype(np.float64) * LOG2E
    c_hi, c_lo = _bf16_split(c.astype(np.float32))
    qaug = np.zeros((DA_HEADS, t // blk, LANE), np.float32)
    qaug[:, :, 0] = c_hi[:, None]
    qaug[:, :, 1] = 256.0 * c_hi[:, None]
    qaug[:, :, 2] = c_lo[:, None]
    qaug[:, :, 3] = 256.0 * c_lo[:, None]
    qaug[:, :, 4] = -(c[:, None] * (np.arange(t // blk)[None, :] * blk))
    return jnp.asarray(kaug.astype(BF16)), jnp.asarray(qaug.astype(BF16))


def _dattn_kernel(q_ref, k_ref, v_ref, kaug_ref, qaug_ref, lamp_ref, gsub_ref, o_ref, *, blk, lam_init):
    t = q_ref.shape[1]
    nq = t // blk
    heads = q_ref.shape[2] // LANE
    kaug = kaug_ref[...]
    ones = jnp.ones((t, LANE), BF16)

    lamp = lamp_ref[...]
    s1 = jnp.sum(lamp[0:1] * lamp[1:2], axis=-1, keepdims=True)
    s2 = jnp.sum(lamp[2:3] * lamp[3:4], axis=-1, keepdims=True)
    lam = jnp.exp(s1) - jnp.exp(s2) + lam_init

    lane = lax.broadcasted_iota(jnp.int32, (blk, LANE), 1)
    ri = lax.broadcasted_iota(jnp.int32, (2 * blk, blk), 0)
    ci = lax.broadcasted_iota(jnp.int32, (2 * blk, blk), 1)
    causal = (ri & (blk - 1)) >= ci
    zero = jnp.zeros((blk, LANE), BF16)

    def scores(h, i):
        hs = slice(h * LANE, (h + 1) * LANE)
        qi = q_ref[0, i * blk:(i + 1) * blk, hs]
        aug = jnp.broadcast_to(qaug_ref[h, i:i + 1, :], (blk, LANE))
        q1 = jnp.concatenate([jnp.where(lane < DA_HEAD_QK, qi, zero), aug], axis=1)
        q2 = jnp.concatenate([jnp.where(lane >= DA_HEAD_QK, qi, zero), aug], axis=1)
        qc = jnp.concatenate([q1, q2], axis=0)
        lo = i * blk
        kd = jnp.concatenate([k_ref[0, lo:lo + blk, hs], kaug[lo:lo + blk]], axis=1)
        sd = lax.dot_general(qc, kd, _NT, preferred_element_type=F32)
        sm = None
        if i > 0:
            km = jnp.concatenate([k_ref[0, :lo, hs], kaug[:lo]], axis=1)
            sm = lax.dot_general(qc, km, _NT, preferred_element_type=F32)
        return sd, sm

    def finish(h, i, sd, sm):
        hs = slice(h * LANE, (h + 1) * LANE)
        lo = i * blk
        sd = jnp.where(causal, sd, -jnp.inf)
        m = jnp.max(sd, axis=-1, keepdims=True)
        if i > 0:
            m = jnp.maximum(m, jnp.max(sm, axis=-1, keepdims=True))
        vd = jnp.concatenate([v_ref[0, lo:lo + blk, hs], ones[:blk]], axis=1)
        o = jnp.dot(jnp.exp2(sd - m).astype(BF16), vd, preferred_element_type=F32)
        if i > 0:
            vm = jnp.concatenate([v_ref[0, :lo, hs], ones[:lo]], axis=1)
            o = o + jnp.dot(jnp.exp2(sm - m).astype(BF16), vm, preferred_element_type=F32)
        on = o[:, :DA_HEAD_V] * (1.0 / o[:, DA_HEAD_V:DA_HEAD_V + 1])
        od = on[:blk] - lam * on[blk:]
        od = _rms(od, gsub_ref[...]) * (1.0 - lam_init)
        o_ref[0, lo:lo + blk, hs] = od.astype(BF16)

    work = [(h, i) for h in range(heads) for i in range(nq)]
    pending = scores(*work[0])
    for n, hi in enumerate(work):
        nxt = scores(*work[n + 1]) if n + 1 < len(work) else None
        finish(*hi, *pending)
        pending = nxt


def _diff_attention(proj, lamp, gsub, layer_idx, blk=256, heads_per_step=4):
    b, t, _ = proj.shape
    kaug, qaug = _alibi_aug(t, blk)
    hps = heads_per_step
    w = hps * LANE
    ngrp = DA_HEADS // hps
    kern = functools.partial(_dattn_kernel, blk=blk, lam_init=_lambda_init(layer_idx))
    return pl.pallas_call(
        kern,
        out_shape=jax.ShapeDtypeStruct((b, t, DA_HEADS * DA_HEAD_V), BF16),
        grid=(b, ngrp),
        in_specs=[
            pl.BlockSpec((1, t, w), lambda bi, h: (bi, 0, h)),
            pl.BlockSpec((1, t, w), lambda bi, h: (bi, 0, ngrp + h)),
            pl.BlockSpec((1, t, w), lambda bi, h: (bi, 0, 2 * ngrp + h)),
            pl.BlockSpec((t, LANE), lambda bi, h: (0, 0)),
            pl.BlockSpec((hps, t // blk, LANE), lambda bi, h: (h, 0, 0)),
            pl.BlockSpec((4, DA_HEAD_QK), lambda bi, h: (0, 0)),
            pl.BlockSpec((1, DA_HEAD_V), lambda bi, h: (0, 0)),
        ],
        out_specs=pl.BlockSpec((1, t, w), lambda bi, h: (bi, 0, h)),
        compiler_params=_cparams(("parallel", "parallel")),
        name="diff_attention",
    )(proj, proj, proj, kaug, qaug, lamp, gsub.reshape(1, DA_HEAD_V))


GLA_FINE_LEVELS = 3


def _gla_cum_matrices(c):
    nlev = int(math.log2(c))
    nfine = min(GLA_FINE_LEVELS, nlev)
    m = np.zeros((nfine + 1, c, c), np.float32)
    for n, lev in enumerate(range(nlev - nfine, nlev)):
        h = c >> (lev + 1)
        for r in range(c):
            mid = (r // (2 * h)) * 2 * h + h
            if r % (2 * h) >= h:
                m[n, r, mid:r + 1] = 1.0
            else:
                m[n, r, r + 1:mid] = 1.0
    for r in range(c):
        m[nfine, r, :r + 1] = 1.0
    return jnp.asarray(m.astype(BF16))


def _gla_kernel(gq_ref, gk_ref, gv_ref, gg_ref, glow_ref, wup_ref, bgk_ref, gnorm_ref, cum_ref,
                o_ref, s_ref, qt_ref, kt_ref, *, c):
    nlev = int(math.log2(c))
    dk, dv, nh = GLA_HEAD_K, GLA_HEAD_V, GLA_HEADS

    @pl.when(pl.program_id(1) == 0)
    def _():
        s_ref[...] = jnp.zeros_like(s_ref)

    x = jnp.dot(glow_ref[0], wup_ref[...], preferred_element_type=F32) + bgk_ref[...]
    logsig = jnp.minimum(x, 0.0) - jnp.log(1.0 + jnp.exp(-jnp.abs(x)))
    la = logsig * (LOG2E / GLA_GATE_NORMALIZER)
    la_hi = la.astype(BF16)
    la_mid = (la - la_hi.astype(F32)).astype(BF16)
    la_lo = (la - la_hi.astype(F32) - la_mid.astype(F32)).astype(BF16)

    def cum(idx, parts):
        mat = cum_ref[idx]
        return sum(jnp.dot(mat, part, preferred_element_type=F32) for part in parts)

    nfine = cum_ref.shape[0] - 1
    sub = 8
    kw = nh * dk
    b3 = cum(nfine, (la_hi, la_mid, la_lo)).reshape(c // sub, sub, kw)

    def last_row(tile, ntiles):
        return jnp.broadcast_to(b3[tile, sub - 1:sub, :][None], (ntiles, sub, kw))

    q = gq_ref[0].astype(F32)
    k = gk_ref[0].astype(F32)
    q3 = q.reshape(c // sub, sub, kw)
    k3 = k.reshape(c // sub, sub, kw)
    sl = lax.broadcasted_iota(jnp.int32, (1, sub, kw), 1)

    for lev in range(nlev):
        h = c >> (lev + 1)
        if h >= sub:
            ht = h // sub
            zero = jnp.zeros((ht, sub, kw), F32)
            qparts, kparts = [], []
            for g in range(c // (2 * h)):
                s_rows = slice(2 * ht * g, 2 * ht * g + ht)
                t_rows = slice(2 * ht * g + ht, 2 * ht * (g + 1))
                mid = last_row(2 * ht * g + ht - 1, ht)
                qparts += [zero, q3[t_rows] * jnp.exp2(b3[t_rows] - mid)]
                kparts += [k3[s_rows] * jnp.exp2(mid - b3[s_rows]), zero]
            qt = jnp.concatenate(qparts, axis=0)
            kt = jnp.concatenate(kparts, axis=0)
        else:
            p = jnp.exp2(cum(lev - (nlev - nfine), (la_hi, la_mid))).reshape(c // sub, sub, kw)
            tside = ((sl >> (nlev - 1 - lev)) & 1) == 1
            qt = q3 * jnp.where(tside, p, 0.0)
            kt = k3 * jnp.where(tside, 0.0, p)
        qt_ref[lev] = qt.reshape(c, kw).astype(BF16)
        kt_ref[lev] = kt.reshape(c, kw).astype(BF16)

    qb = (q * jnp.exp2(b3.reshape(c, kw))).astype(BF16)
    kr = (k * jnp.exp2((last_row(c // sub - 1, c // sub) - b3).reshape(c, kw))).astype(BF16)
    qk = q * k

    tx = lax.broadcasted_iota(jnp.int32, (c, c), 0) ^ lax.broadcasted_iota(jnp.int32, (c, c), 1)
    ones = jnp.ones((c, LANE), BF16)

    def level_scores(hh):
        ks = slice(hh * dk, (hh + 1) * dk)
        return [lax.dot_general(qt_ref[lev, :, ks], kt_ref[lev, :, ks], _NT, preferred_element_type=F32)
                for lev in range(nlev)]

    pending = level_scores(0)
    for hh in range(nh):
        ks = slice(hh * dk, (hh + 1) * dk)
        vs = slice(hh * dv, (hh + 1) * dv)
        als = pending
        pending = level_scores(hh + 1) if hh + 1 < nh else None
        a = als[0]
        for lev in range(1, nlev):
            a = jnp.where(tx < (1 << (nlev - lev)), als[lev], a)
        v = gv_ref[0, :, vs]
        s_old = s_ref[hh]
        diag = jnp.sum(qk[:, ks], axis=-1, keepdims=True)
        o = (jnp.dot(a.astype(BF16), v, preferred_element_type=F32)
             + jnp.dot(qb[:, ks], s_old.astype(BF16), preferred_element_type=F32)
             + diag * v.astype(F32))
        tot = (lax.dot_general(la_hi[:, ks], ones, _TN, preferred_element_type=F32)
               + lax.dot_general(la_mid[:, ks], ones, _TN, preferred_element_type=F32))
        dec = jnp.exp2(tot)
        dec = jnp.concatenate([dec] * (dv // LANE), axis=1)
        s_ref[hh] = dec * s_old + lax.dot_general(kr[:, ks], v, _TN, preferred_element_type=F32)
        on = _rms(o, gnorm_ref[...])
        g = gg_ref[0, :, vs].astype(F32)
        o_ref[0, :, vs] = (on * (g * jax.nn.sigmoid(g))).astype(BF16)


def _gla(proj, wup, bgk, gnorm, c=256):
    b, t, _ = proj.shape
    nlev = int(math.log2(c))
    kw = GLA_HEADS * GLA_HEAD_K
    vw = GLA_HEADS * GLA_HEAD_V
    q_off = 3 * DA_HEADS * DA_HEAD_V
    cum = _gla_cum_matrices(c)
    kern = functools.partial(_gla_kernel, c=c)
    return pl.pallas_call(
        kern,
        out_shape=jax.ShapeDtypeStruct((b, t, vw), BF16),
        grid=(b, t // c),
        in_specs=[
            pl.BlockSpec((1, c, kw), lambda bi, i: (bi, i, q_off // kw)),
            pl.BlockSpec((1, c, kw), lambda bi, i: (bi, i, q_off // kw + 1)),
            pl.BlockSpec((1, c, vw), lambda bi, i: (bi, i, (q_off + 2 * kw) // vw)),
            pl.BlockSpec((1, c, vw), lambda bi, i: (bi, i, (q_off + 2 * kw) // vw + 1)),
            pl.BlockSpec((1, c, LANE), lambda bi, i: (bi, i, (q_off + 2 * kw + 2 * vw) // LANE)),
            pl.BlockSpec((LANE, kw), lambda bi, i: (0, 0)),
            pl.BlockSpec((1, kw), lambda bi, i: (0, 0)),
            pl.BlockSpec((1, GLA_HEAD_V), lambda bi, i: (0, 0)),
            pl.BlockSpec(cum.shape, lambda bi, i: (0, 0, 0)),
        ],
        out_specs=pl.BlockSpec((1, c, vw), lambda bi, i: (bi, i, 0)),
        scratch_shapes=[
            pltpu.VMEM((GLA_HEADS, GLA_HEAD_K, GLA_HEAD_V), F32),
            pltpu.VMEM((nlev, c, kw), BF16),
            pltpu.VMEM((nlev, c, kw), BF16),
        ],
        compiler_params=_cparams(("parallel", "arbitrary")),
        name="gla",
    )(proj, proj, proj, proj, proj, wup, bgk.reshape(1, kw), gnorm.reshape(1, GLA_HEAD_V), cum)


def _outproj_kernel(od_ref, og_ref, w1_ref, w2_ref, x_ref, mod_ref, g_ref, x1_ref, h2_ref):
    mix = (jnp.dot(od_ref[...], w1_ref[...], preferred_element_type=F32)
           + jnp.dot(og_ref[...], w2_ref[...], preferred_element_type=F32))
    x1 = x_ref[...] + mod_ref[0, 2:3, :] * mix
    x1_ref[...] = x1
    y = _rms(x1, g_ref[...])
    h2_ref[...] = (y * (1.0 + mod_ref[0, 4:5, :]) + mod_ref[0, 3:4, :]).astype(BF16)


def _out_projection(od, og, w_o, layer, x2d, mod, g, seq, tm=512):
    n, d = x2d.shape
    half = od.shape[1]
    tiles_per_seq = seq // tm
    return pl.pallas_call(
        _outproj_kernel,
        out_shape=(jax.ShapeDtypeStruct((n, d), F32), jax.ShapeDtypeStruct((n, d), BF16)),
        grid=(n // tm,),
        in_specs=[
            pl.BlockSpec((tm, half), lambda i: (i, 0)),
            pl.BlockSpec((tm, half), lambda i: (i, 0)),
            pl.BlockSpec((half, d), lambda i: (2 * layer, 0)),
            pl.BlockSpec((half, d), lambda i: (2 * layer + 1, 0)),
            pl.BlockSpec((tm, d), lambda i: (i, 0)),
            pl.BlockSpec((1, N_MOD, d), lambda i: (i // tiles_per_seq, 0, 0)),
            pl.BlockSpec((1, d), lambda i: (0, 0)),
        ],
        out_specs=(pl.BlockSpec((tm, d), lambda i: (i, 0)), pl.BlockSpec((tm, d), lambda i: (i, 0))),
        compiler_params=_cparams(("parallel",)),
        name="out_proj_residual_norm",
    )(od, og, w_o, w_o, x2d, mod, g.reshape(1, d))


def _ffn_kernel(h2_ref, wu_ref, wg_ref, wc_ref, bc_ref, wd_ref, x1_hbm, mod_ref, gfin_ref,
                o_ref, x1_buf, x1_sem, carry_ref, *, tiles_per_seq, final):
    i = pl.program_id(0)
    j = pl.program_id(1)
    last = pl.num_programs(1) - 1
    tm = h2_ref.shape[0]
    tf = wu_ref.shape[1]
    sub = 8

    def x1_copy():
        return pltpu.make_async_copy(x1_hbm.at[pl.ds(pl.multiple_of(i * tm, tm), tm), :], x1_buf, x1_sem)

    @pl.when(j == 0)
    def _():
        x1_copy().start()

    def down_partial():
        h2 = h2_ref[...]
        g = jnp.dot(h2, wg_ref[...], preferred_element_type=F32)
        u = jnp.dot(h2, wu_ref[...], preferred_element_type=F32)
        g3 = g.reshape(tm // sub, sub, tf)

        prev = carry_ref[j]
        prev = jnp.where(i % tiles_per_seq == 0, jnp.zeros_like(prev), prev)
        carry_ref[j] = g3[tm // sub - 1]
        sl = lax.broadcasted_iota(jnp.int32, (1, sub, tf), 1)

        def shifted(n):
            rot = pltpu.roll(g3, n, 1)
            before = jnp.concatenate([pltpu.roll(prev[None], n, 1), rot[:-1]], axis=0)
            return jnp.where(sl < n, before, rot)

        wc = wc_ref[...] * SQRT_HALF
        z = bc_ref[...] * SQRT_HALF + shifted(2) * wc[0:1] + shifted(1) * wc[1:2] + g3 * wc[2:3]
        act = (z * (1.0 + lax.erf(z))).reshape(tm, tf) * u
        return jnp.dot(act.astype(BF16), wd_ref[...], preferred_element_type=F32)

    @pl.when(j == 0)
    def _():
        o_ref[...] = down_partial()

    @pl.when(j > 0)
    def _():
        o_ref[...] += down_partial()

    @pl.when(j == last)
    def _():
        x1_copy().wait()
        x2 = x1_buf[...] + (mod_ref[0, 5:6, :] * SQRT_HALF) * o_ref[...]
        if final:
            x2 = _rms(x2, gfin_ref[...])
        o_ref[...] = x2


def _ffn(h2, w_up, w_conv, b_conv, w_down, layer, x1, mod, g_final, seq, final, tm=1024, tf=512):
    n, d = x1.shape
    f = w_conv.shape[1]
    nf = f // tf
    tm = min(tm, seq)
    tiles_per_seq = seq // tm
    kern = functools.partial(_ffn_kernel, tiles_per_seq=tiles_per_seq, final=final)
    return pl.pallas_call(
        kern,
        out_shape=jax.ShapeDtypeStruct((n, d), F32),
        grid=(n // tm, nf),
        in_specs=[
            pl.BlockSpec((tm, d), lambda i, j: (i, 0)),
            pl.BlockSpec((d, tf), lambda i, j: (layer, j)),
            pl.BlockSpec((d, tf), lambda i, j: (layer, nf + j)),
            pl.BlockSpec((CONV_WIDTH, tf), lambda i, j: (0, j)),
            pl.BlockSpec((1, tf), lambda i, j: (0, j)),
            pl.BlockSpec((tf, d), lambda i, j: (layer * nf + j, 0)),
            pl.BlockSpec(memory_space=pl.ANY),
            pl.BlockSpec((1, N_MOD, d), lambda i, j: (i // tiles_per_seq, 0, 0)),
            pl.BlockSpec((1, d), lambda i, j: (0, 0)),
        ],
        out_specs=pl.BlockSpec((tm, d), lambda i, j: (i, 0)),
        scratch_shapes=[
            pltpu.VMEM((tm, d), F32),
            pltpu.SemaphoreType.DMA,
            pltpu.VMEM((nf, 8, tf), F32),
        ],
        compiler_params=_cparams(("arbitrary", "arbitrary")),
        name="conv_ffn_residual",
    )(h2, w_up, w_up, w_conv, b_conv.reshape(1, f), w_down, x1, mod, g_final.reshape(1, d))


def kernel(x, c, w_ada, b_ada, g_mix_norm, w_in, diff_lambda, g_diff_subln, w_gk_up, b_gk,
           g_gla_norm, w_o, g_ffn_norm, w_up, w_conv, b_conv, w_down, g_final):
    b, t, d = x.shape
    depth = w_ada.shape[0]
    n_proj = w_in.shape[2]
    in_tile = 1280
    n_proj_pad = -(-n_proj // in_tile) * in_tile
    col_scale = np.ones((1, n_proj_pad), np.float32)
    da_w = DA_HEADS * DA_HEAD_V
    col_scale[0, :da_w] = DA_HEAD_QK ** -0.5 * LOG2E
    col_scale[0, 3 * da_w:3 * da_w + GLA_HEADS * GLA_HEAD_K] = GLA_HEAD_K ** -0.5
    col_scale = jnp.asarray(col_scale)

    w_in_b = jnp.pad(w_in.astype(BF16), ((0, 0), (0, 0), (0, n_proj_pad - n_proj))).reshape(depth * d, -1)
    w_o_b = w_o.astype(BF16).reshape(-1, d)
    w_up_b = w_up.astype(BF16).reshape(depth * d, -1)
    w_down_b = w_down.astype(BF16).reshape(-1, d)

    mod_all = _modulation(c, w_ada, b_ada).reshape(depth, b, N_MOD, d)
    x2d = x.reshape(b * t, d)
    for l in range(depth):
        mod = mod_all[l]
        wup_l = jnp.pad(w_gk_up[l].astype(BF16), ((0, LANE - GLA_GATE_RANK), (0, 0)))

        proj = _in_projection(x2d, mod, g_mix_norm[l], w_in_b, l, col_scale, t, tn=in_tile)
        proj = proj.reshape(b, t, n_proj_pad)
        od = _diff_attention(proj, diff_lambda[l], g_diff_subln[l], l)
        og = _gla(proj, wup_l, b_gk[l], g_gla_norm[l])
        x1, h2 = _out_projection(od.reshape(b * t, -1), og.reshape(b * t, -1), w_o_b, l,
                                 x2d, mod, g_ffn_norm[l], t)
        x2d = _ffn(h2, w_up_b, w_conv[l], b_conv[l], w_down_b, l, x1, mod,
                   g_final, t, final=(l == depth - 1))
    return x2d.reshape(b, t, d)
```

```python
import functools
import math

import numpy as np
import jax
import jax.numpy as jnp
from jax import lax
from jax.experimental import pallas as pl
from jax.experimental.pallas import tpu as pltpu

BF16 = jnp.bfloat16
F32 = jnp.float32

NORM_EPS = 1e-6
N_MOD = 6
DA_HEADS = 8
DA_HEAD_V = 128
DA_HEAD_QK = 64
GLA_HEADS = 4
GLA_HEAD_K = 128
GLA_HEAD_V = 256
GLA_GATE_RANK = 16
GLA_GATE_NORMALIZER = 16.0
CONV_WIDTH = 3
GLA_PROJ_WIDTH = GLA_HEADS * (2 * GLA_HEAD_K + 2 * GLA_HEAD_V)

LANE = 128
V7X_VMEM_LIMIT = 56 * 1024 * 1024
LOG2E = 1.4426950408889634
SQRT_HALF = 0.7071067811865476

_NT = (((1,), (1,)), ((), ()))
_TN = (((0,), (0,)), ((), ()))


def _cparams(sem):
    return pltpu.CompilerParams(dimension_semantics=sem, vmem_limit_bytes=V7X_VMEM_LIMIT)


def _lambda_init(layer_idx):
    return 0.8 - 0.6 * math.exp(-0.3 * layer_idx)


def _rms(x, g):
    ms = jnp.mean(x * x, axis=-1, keepdims=True)
    return x * lax.rsqrt(ms + NORM_EPS) * g


def _mod_kernel(c_ref, w_ref, b_ref, o_ref):
    c = c_ref[...]
    ca = (c * jax.nn.sigmoid(c)).astype(BF16)
    w = w_ref[0].astype(BF16)
    o_ref[0] = jnp.dot(ca, w, preferred_element_type=F32) + b_ref[0]


def _modulation(c, w_ada, b_ada, tn=1024):
    depth, d, n = w_ada.shape
    b = c.shape[0]
    return pl.pallas_call(
        _mod_kernel,
        out_shape=jax.ShapeDtypeStruct((depth, b, n), F32),
        grid=(depth, n // tn),
        in_specs=[
            pl.BlockSpec((b, d), lambda l, j: (0, 0)),
            pl.BlockSpec((1, d, tn), lambda l, j: (l, 0, j)),
            pl.BlockSpec((1, 1, tn), lambda l, j: (l, 0, j)),
        ],
        out_specs=pl.BlockSpec((1, b, tn), lambda l, j: (l, 0, j)),
        compiler_params=_cparams(("parallel", "parallel")),
        name="adaln_mod",
    )(c, w_ada, b_ada.reshape(depth, 1, n))


def _inproj_kernel(x_ref, mod_ref, g_ref, w_ref, cs_ref, o_ref, h_ref, *, nchunk):
    i = pl.program_id(0)
    j = pl.program_id(1)
    cr = x_ref.shape[0] // nchunk

    def norm_chunk():
        c = jnp.minimum(j, nchunk - 1)
        rows = pl.ds(pl.multiple_of(c * cr, cr), cr)
        y = _rms(x_ref[rows, :], g_ref[...])
        h_ref[i % 2, rows, :] = (y * (1.0 + mod_ref[0, 1:2, :]) + mod_ref[0, 0:1, :]).astype(BF16)

    @pl.when(i == 0)
    def _():
        norm_chunk()

    @pl.when(i > 0)
    def _():
        acc = jnp.dot(h_ref[(i + 1) % 2], w_ref[...], preferred_element_type=F32)
        o_ref[...] = (acc * cs_ref[...]).astype(BF16)
        norm_chunk()


def _in_projection(x2d, mod, g, w, layer, col_scale, seq, tm=1024, tn=1280):
    n, d = x2d.shape
    p = w.shape[1]
    tm = min(tm, seq)
    tiles_per_seq = seq // tm
    nt = n // tm
    nj = p // tn
    nchunk = max(c for c in range(1, nj + 1) if tm % (16 * c) == 0)
    col = lambda i, j: jnp.where(i == 0, 0, j)
    return pl.pallas_call(
        functools.partial(_inproj_kernel, nchunk=nchunk),
        out_shape=jax.ShapeDtypeStruct((n, p), BF16),
        grid=(nt + 1, nj),
        in_specs=[
            pl.BlockSpec((tm, d), lambda i, j: (jnp.minimum(i, nt - 1), 0)),
            pl.BlockSpec((1, N_MOD, d), lambda i, j: (jnp.minimum(i, nt - 1) // tiles_per_seq, 0, 0)),
            pl.BlockSpec((1, d), lambda i, j: (0, 0)),
            pl.BlockSpec((d, tn), lambda i, j: (layer, col(i, j))),
            pl.BlockSpec((1, tn), lambda i, j: (0, col(i, j))),
        ],
        out_specs=pl.BlockSpec((tm, tn), lambda i, j: (jnp.maximum(i - 1, 0), col(i, j))),
        scratch_shapes=[pltpu.VMEM((2, tm, d), BF16)],
        compiler_params=_cparams(("arbitrary", "arbitrary")),
        name="norm_in_proj",
    )(x2d, mod, g.reshape(1, d), w, col_scale)


def _alibi_slopes(n):
    start = 2.0 ** (-8.0 / n)
    return np.array([start ** (i + 1) for i in range(n)], dtype=np.float32)


def _bf16_split(x):
    hi = x.astype(BF16).astype(np.float32)
    lo = (x - hi).astype(BF16).astype(np.float32)
    return hi, lo


def _alibi_aug(t, blk):
    j = np.arange(t)
    kaug = np.zeros((t, LANE), np.float32)
    kaug[:, 0] = kaug[:, 2] = j % 256
    kaug[:, 1] = kaug[:, 3] = j // 256
    kaug[:, 4] = 1.0
    c = _alibi_slopes(DA_HEADS).astype(np.float64) * LOG2E
    c_hi, c_lo = _bf16_split(c.astype(np.float32))
    qaug = np.zeros((DA_HEADS, t // blk, LANE), np.float32)
    qaug[:, :, 0] = c_hi[:, None]
    qaug[:, :, 1] = 256.0 * c_hi[:, None]
    qaug[:, :, 2] = c_lo[:, None]
    qaug[:, :, 3] = 256.0 * c_lo[:, None]
    qaug[:, :, 4] = -(c[:, None] * (np.arange(t // blk)[None, :] * blk))
    return jnp.asarray(kaug.astype(BF16)), jnp.asarray(qaug.astype(BF16))


def _dattn_kernel(q_ref, k_ref, v_ref, kaug_ref, qaug_ref, lamp_ref, gsub_ref, o_ref, *, blk, lam_init):
    t = q_ref.shape[1]
    nq = t // blk
    heads = q_ref.shape[2] // LANE
    kaug = kaug_ref[...]
    ones = jnp.ones((t, LANE), BF16)

    lamp = lamp_ref[...]
    s1 = jnp.sum(lamp[0:1] * lamp[1:2], axis=-1, keepdims=True)
    s2 = jnp.sum(lamp[2:3] * lamp[3:4], axis=-1, keepdims=True)
    lam = jnp.exp(s1) - jnp.exp(s2) + lam_init

    lane = lax.broadcasted_iota(jnp.int32, (blk, LANE), 1)
    ri = lax.broadcasted_iota(jnp.int32, (2 * blk, blk), 0)
    ci = lax.broadcasted_iota(jnp.int32, (2 * blk, blk), 1)
    causal = (ri & (blk - 1)) >= ci
    zero = jnp.zeros((blk, LANE), BF16)

    def scores(h, i):
        hs = slice(h * LANE, (h + 1) * LANE)
        qi = q_ref[0, i * blk:(i + 1) * blk, hs]
        aug = jnp.broadcast_to(qaug_ref[h, i:i + 1, :], (blk, LANE))
        q1 = jnp.concatenate([jnp.where(lane < DA_HEAD_QK, qi, zero), aug], axis=1)
        q2 = jnp.concatenate([jnp.where(lane >= DA_HEAD_QK, qi, zero), aug], axis=1)
        qc = jnp.concatenate([q1, q2], axis=0)
        lo = i * blk
        kd = jnp.concatenate([k_ref[0, lo:lo + blk, hs], kaug[lo:lo + blk]], axis=1)
        sd = lax.dot_general(qc, kd, _NT, preferred_element_type=F32)
        sm = None
        if i > 0:
            km = jnp.concatenate([k_ref[0, :lo, hs], kaug[:lo]], axis=1)
            sm = lax.dot_general(qc, km, _NT, preferred_element_type=F32)
        return sd, sm

    def finish(h, i, sd, sm):
        hs = slice(h * LANE, (h + 1) * LANE)
        lo = i * blk
        sd = jnp.where(causal, sd, -jnp.inf)
        m = jnp.max(sd, axis=-1, keepdims=True)
        if i > 0:
            m = jnp.maximum(m, jnp.max(sm, axis=-1, keepdims=True))
        vd = jnp.concatenate([v_ref[0, lo:lo + blk, hs], ones[:blk]], axis=1)
        o = jnp.dot(jnp.exp2(sd - m).astype(BF16), vd, preferred_element_type=F32)
        if i > 0:
            vm = jnp.concatenate([v_ref[0, :lo, hs], ones[:lo]], axis=1)
            o = o + jnp.dot(jnp.exp2(sm - m).astype(BF16), vm, preferred_element_type=F32)
        on = o[:, :DA_HEAD_V] * (1.0 / o[:, DA_HEAD_V:DA_HEAD_V + 1])
        od = on[:blk] - lam * on[blk:]
        od = _rms(od, gsub_ref[...]) * (1.0 - lam_init)
        o_ref[0, lo:lo + blk, hs] = od.astype(BF16)

    work = [(h, i) for h in range(heads) for i in range(nq)]
    pending = scores(*work[0])
    for n, hi in enumerate(work):
        nxt = scores(*work[n + 1]) if n + 1 < len(work) else None
        finish(*hi, *pending)
        pending = nxt


def _diff_attention(proj, lamp, gsub, layer_idx, blk=256, heads_per_step=4):
    b, t, _ = proj.shape
    kaug, qaug = _alibi_aug(t, blk)
    hps = heads_per_step
    w = hps * LANE
    ngrp = DA_HEADS // hps
    base = GLA_PROJ_WIDTH // w
    kern = functools.partial(_dattn_kernel, blk=blk, lam_init=_lambda_init(layer_idx))
    return pl.pallas_call(
        kern,
        out_shape=jax.ShapeDtypeStruct((b, t, DA_HEADS * DA_HEAD_V), BF16),
        grid=(b, ngrp),
        in_specs=[
            pl.BlockSpec((1, t, w), lambda bi, h: (bi, 0, base + h)),
            pl.BlockSpec((1, t, w), lambda bi, h: (bi, 0, base + ngrp + h)),
            pl.BlockSpec((1, t, w), lambda bi, h: (bi, 0, base + 2 * ngrp + h)),
            pl.BlockSpec((t, LANE), lambda bi, h: (0, 0)),
            pl.BlockSpec((hps, t // blk, LANE), lambda bi, h: (h, 0, 0)),
            pl.BlockSpec((4, DA_HEAD_QK), lambda bi, h: (0, 0)),
            pl.BlockSpec((1, DA_HEAD_V), lambda bi, h: (0, 0)),
        ],
        out_specs=pl.BlockSpec((1, t, w), lambda bi, h: (bi, 0, h)),
        compiler_params=_cparams(("parallel", "parallel")),
        name="diff_attention",
    )(proj, proj, proj, kaug, qaug, lamp, gsub.reshape(1, DA_HEAD_V))


GLA_FINE_LEVELS = 3


def _gla_cum_matrices(c):
    nlev = int(math.log2(c))
    nfine = min(GLA_FINE_LEVELS, nlev)
    m = np.zeros((nfine + 1, c, c), np.float32)
    for n, lev in enumerate(range(nlev - nfine, nlev)):
        h = c >> (lev + 1)
        for r in range(c):
            mid = (r // (2 * h)) * 2 * h + h
            if r % (2 * h) >= h:
                m[n, r, mid:r + 1] = 1.0
            else:
                m[n, r, r + 1:mid] = 1.0
    for r in range(c):
        m[nfine, r, :r + 1] = 1.0
    return jnp.asarray(m.astype(BF16))


def _gla_kernel(main_ref, glow_ref, wup_ref, bgk_ref, gnorm_ref, cum_ref,
                o_ref, s_ref, qt_ref, kt_ref, *, c):
    nlev = int(math.log2(c))
    dk, dv, nh = GLA_HEAD_K, GLA_HEAD_V, GLA_HEADS

    @pl.when(pl.program_id(1) == 0)
    def _():
        s_ref[...] = jnp.zeros_like(s_ref)

    x = jnp.dot(glow_ref[0], wup_ref[...], preferred_element_type=F32) + bgk_ref[...]
    logsig = jnp.minimum(x, 0.0) - jnp.log(1.0 + jnp.exp(-jnp.abs(x)))
    la = logsig * (LOG2E / GLA_GATE_NORMALIZER)
    la_hi = la.astype(BF16)
    la_mid = (la - la_hi.astype(F32)).astype(BF16)
    la_lo = (la - la_hi.astype(F32) - la_mid.astype(F32)).astype(BF16)

    def cum(idx, parts):
        mat = cum_ref[idx]
        return sum(jnp.dot(mat, part, preferred_element_type=F32) for part in parts)

    nfine = cum_ref.shape[0] - 1
    sub = 8
    kw = nh * dk
    b3 = cum(nfine, (la_hi, la_mid, la_lo)).reshape(c // sub, sub, kw)

    def last_row(tile, ntiles):
        return jnp.broadcast_to(b3[tile, sub - 1:sub, :][None], (ntiles, sub, kw))

    q = main_ref[0, :, 0:kw].astype(F32)
    k = main_ref[0, :, kw:2 * kw].astype(F32)
    q3 = q.reshape(c // sub, sub, kw)
    k3 = k.reshape(c // sub, sub, kw)
    sl = lax.broadcasted_iota(jnp.int32, (1, sub, kw), 1)

    for lev in range(nlev):
        h = c >> (lev + 1)
        if h >= sub:
            ht = h // sub
            zero = jnp.zeros((ht, sub, kw), F32)
            qparts, kparts = [], []
            for g in range(c // (2 * h)):
                s_rows = slice(2 * ht * g, 2 * ht * g + ht)
                t_rows = slice(2 * ht * g + ht, 2 * ht * (g + 1))
                mid = last_row(2 * ht * g + ht - 1, ht)
                qparts += [zero, q3[t_rows] * jnp.exp2(b3[t_rows] - mid)]
                kparts += [k3[s_rows] * jnp.exp2(mid - b3[s_rows]), zero]
            qt = jnp.concatenate(qparts, axis=0)
            kt = jnp.concatenate(kparts, axis=0)
        else:
            p = jnp.exp2(cum(lev - (nlev - nfine), (la_hi, la_mid))).reshape(c // sub, sub, kw)
            tside = ((sl >> (nlev - 1 - lev)) & 1) == 1
            qt = q3 * jnp.where(tside, p, 0.0)
            kt = k3 * jnp.where(tside, 0.0, p)
        qt_ref[lev] = qt.reshape(c, kw).astype(BF16)
        kt_ref[lev] = kt.reshape(c, kw).astype(BF16)

    qb = (q * jnp.exp2(b3.reshape(c, kw))).astype(BF16)
    kr = (k * jnp.exp2((last_row(c // sub - 1, c // sub) - b3).reshape(c, kw))).astype(BF16)
    qk = q * k

    tx = lax.broadcasted_iota(jnp.int32, (c, c), 0) ^ lax.broadcasted_iota(jnp.int32, (c, c), 1)
    ones = jnp.ones((c, LANE), BF16)

    def level_scores(hh):
        ks = slice(hh * dk, (hh + 1) * dk)
        return [lax.dot_general(qt_ref[lev, :, ks], kt_ref[lev, :, ks], _NT, preferred_element_type=F32)
                for lev in range(nlev)]

    pending = level_scores(0)
    for hh in range(nh):
        ks = slice(hh * dk, (hh + 1) * dk)
        vs = slice(hh * dv, (hh + 1) * dv)
        als = pending
        pending = level_scores(hh + 1) if hh + 1 < nh else None
        a = als[0]
        for lev in range(1, nlev):
            a = jnp.where(tx < (1 << (nlev - lev)), als[lev], a)
        v = main_ref[0, :, 2 * kw + hh * dv:2 * kw + (hh + 1) * dv]
        s_old = s_ref[hh]
        diag = jnp.sum(qk[:, ks], axis=-1, keepdims=True)
        o = (jnp.dot(a.astype(BF16), v, preferred_element_type=F32)
             + jnp.dot(qb[:, ks], s_old.astype(BF16), preferred_element_type=F32)
             + diag * v.astype(F32))
        tot = (lax.dot_general(la_hi[:, ks], ones, _TN, preferred_element_type=F32)
               + lax.dot_general(la_mid[:, ks], ones, _TN, preferred_element_type=F32))
        dec = jnp.exp2(tot)
        dec = jnp.concatenate([dec] * (dv // LANE), axis=1)
        s_ref[hh] = dec * s_old + lax.dot_general(kr[:, ks], v, _TN, preferred_element_type=F32)
        on = _rms(o, gnorm_ref[...])
        g = main_ref[0, :, 2 * kw + nh * dv + hh * dv:2 * kw + nh * dv + (hh + 1) * dv].astype(F32)
        o_ref[0, :, vs] = (on * (g * jax.nn.sigmoid(g))).astype(BF16)


def _gla(proj, wup, bgk, gnorm, c=256):
    b, t, _ = proj.shape
    nlev = int(math.log2(c))
    kw = GLA_HEADS * GLA_HEAD_K
    vw = GLA_HEADS * GLA_HEAD_V
    main_w = 2 * kw + 2 * vw
    glow_off = main_w + 3 * DA_HEADS * DA_HEAD_V
    cum = _gla_cum_matrices(c)
    kern = functools.partial(_gla_kernel, c=c)
    return pl.pallas_call(
        kern,
        out_shape=jax.ShapeDtypeStruct((b, t, vw), BF16),
        grid=(b, t // c),
        in_specs=[
            pl.BlockSpec((1, c, main_w), lambda bi, i: (bi, i, 0)),
            pl.BlockSpec((1, c, LANE), lambda bi, i: (bi, i, glow_off // LANE)),
            pl.BlockSpec((LANE, kw), lambda bi, i: (0, 0)),
            pl.BlockSpec((1, kw), lambda bi, i: (0, 0)),
            pl.BlockSpec((1, GLA_HEAD_V), lambda bi, i: (0, 0)),
            pl.BlockSpec(cum.shape, lambda bi, i: (0, 0, 0)),
        ],
        out_specs=pl.BlockSpec((1, c, vw), lambda bi, i: (bi, i, 0)),
        scratch_shapes=[
            pltpu.VMEM((GLA_HEADS, GLA_HEAD_K, GLA_HEAD_V), F32),
            pltpu.VMEM((nlev, c, kw), BF16),
            pltpu.VMEM((nlev, c, kw), BF16),
        ],
        compiler_params=_cparams(("parallel", "arbitrary")),
        name="gla",
    )(proj, proj, wup, bgk.reshape(1, kw), gnorm.reshape(1, GLA_HEAD_V), cum)


def _outproj_kernel(od_ref, og_ref, w1_ref, w2_ref, x_ref, mod_ref, g_ref, x1_ref, h2_ref):
    mix = (jnp.dot(od_ref[...], w1_ref[...], preferred_element_type=F32)
           + jnp.dot(og_ref[...], w2_ref[...], preferred_element_type=F32))
    x1 = x_ref[...] + mod_ref[0, 2:3, :] * mix
    x1_ref[...] = x1
    y = _rms(x1, g_ref[...])
    h2_ref[...] = (y * (1.0 + mod_ref[0, 4:5, :]) + mod_ref[0, 3:4, :]).astype(BF16)


def _out_projection(od, og, w_o, layer, x2d, mod, g, seq, tm=512):
    n, d = x2d.shape
    half = od.shape[1]
    tiles_per_seq = seq // tm
    return pl.pallas_call(
        _outproj_kernel,
        out_shape=(jax.ShapeDtypeStruct((n, d), F32), jax.ShapeDtypeStruct((n, d), BF16)),
        grid=(n // tm,),
        in_specs=[
            pl.BlockSpec((tm, half), lambda i: (i, 0)),
            pl.BlockSpec((tm, half), lambda i: (i, 0)),
            pl.BlockSpec((half, d), lambda i: (2 * layer, 0)),
            pl.BlockSpec((half, d), lambda i: (2 * layer + 1, 0)),
            pl.BlockSpec((tm, d), lambda i: (i, 0)),
            pl.BlockSpec((1, N_MOD, d), lambda i: (i // tiles_per_seq, 0, 0)),
            pl.BlockSpec((1, d), lambda i: (0, 0)),
        ],
        out_specs=(pl.BlockSpec((tm, d), lambda i: (i, 0)), pl.BlockSpec((tm, d), lambda i: (i, 0))),
        compiler_params=_cparams(("parallel",)),
        name="out_proj_residual_norm",
    )(od, og, w_o, w_o, x2d, mod, g.reshape(1, d))


def _ffn_kernel(h2_ref, wu_ref, wg_ref, wc_ref, bc_ref, wd_ref, x1_hbm, mod_ref, gfin_ref,
                o_ref, x1_buf, x1_sem, carry_ref, *, tiles_per_seq, final):
    i = pl.program_id(0)
    j = pl.program_id(1)
    last = pl.num_programs(1) - 1
    tm = h2_ref.shape[0]
    tf = wu_ref.shape[1]
    sub = 8

    def x1_copy():
        return pltpu.make_async_copy(x1_hbm.at[pl.ds(pl.multiple_of(i * tm, tm), tm), :], x1_buf, x1_sem)

    @pl.when(j == 0)
    def _():
        x1_copy().start()

    def down_partial():
        h2 = h2_ref[...]
        g = jnp.dot(h2, wg_ref[...], preferred_element_type=F32)
        u = jnp.dot(h2, wu_ref[...], preferred_element_type=F32)
        g3 = g.reshape(tm // sub, sub, tf)

        prev = carry_ref[j]
        prev = jnp.where(i % tiles_per_seq == 0, jnp.zeros_like(prev), prev)
        carry_ref[j] = g3[tm // sub - 1]
        sl = lax.broadcasted_iota(jnp.int32, (1, sub, tf), 1)

        def shifted(n):
            rot = pltpu.roll(g3, n, 1)
            before = jnp.concatenate([pltpu.roll(prev[None], n, 1), rot[:-1]], axis=0)
            return jnp.where(sl < n, before, rot)

        wc = wc_ref[...] * SQRT_HALF
        z = bc_ref[...] * SQRT_HALF + shifted(2) * wc[0:1] + shifted(1) * wc[1:2] + g3 * wc[2:3]
        act = (z * (1.0 + lax.erf(z))).reshape(tm, tf) * u
        return jnp.dot(act.astype(BF16), wd_ref[...], preferred_element_type=F32)

    @pl.when(j == 0)
    def _():
        o_ref[...] = down_partial()

    @pl.when(j > 0)
    def _():
        o_ref[...] += down_partial()

    @pl.when(j == last)
    def _():
        x1_copy().wait()
        x2 = x1_buf[...] + (mod_ref[0, 5:6, :] * SQRT_HALF) * o_ref[...]
        if final:
            x2 = _rms(x2, gfin_ref[...])
        o_ref[...] = x2


def _ffn(h2, w_up, w_conv, b_conv, w_down, layer, x1, mod, g_final, seq, final, tm=1024, tf=512):
    n, d = x1.shape
    f = w_conv.shape[1]
    nf = f // tf
    tm = min(tm, seq)
    tiles_per_seq = seq // tm
    kern = functools.partial(_ffn_kernel, tiles_per_seq=tiles_per_seq, final=final)
    return pl.pallas_call(
        kern,
        out_shape=jax.ShapeDtypeStruct((n, d), F32),
        grid=(n // tm, nf),
        in_specs=[
            pl.BlockSpec((tm, d), lambda i, j: (i, 0)),
            pl.BlockSpec((d, tf), lambda i, j: (layer, j)),
            pl.BlockSpec((d, tf), lambda i, j: (layer, nf + j)),
            pl.BlockSpec((CONV_WIDTH, tf), lambda i, j: (0, j)),
            pl.BlockSpec((1, tf), lambda i, j: (0, j)),
            pl.BlockSpec((tf, d), lambda i, j: (layer * nf + j, 0)),
            pl.BlockSpec(memory_space=pl.ANY),
            pl.BlockSpec((1, N_MOD, d), lambda i, j: (i // tiles_per_seq, 0, 0)),
            pl.BlockSpec((1, d), lambda i, j: (0, 0)),
        ],
        out_specs=pl.BlockSpec((tm, d), lambda i, j: (i, 0)),
        scratch_shapes=[
            pltpu.VMEM((tm, d), F32),
            pltpu.SemaphoreType.DMA,
            pltpu.VMEM((nf, 8, tf), F32),
        ],
        compiler_params=_cparams(("arbitrary", "arbitrary")),
        name="conv_ffn_residual",
    )(h2, w_up, w_up, w_conv, b_conv.reshape(1, f), w_down, x1, mod, g_final.reshape(1, d))


def kernel(x, c, w_ada, b_ada, g_mix_norm, w_in, diff_lambda, g_diff_subln, w_gk_up, b_gk,
           g_gla_norm, w_o, g_ffn_norm, w_up, w_conv, b_conv, w_down, g_final):
    b, t, d = x.shape
    depth = w_ada.shape[0]
    n_proj = w_in.shape[2]
    in_tile = 1280
    n_proj_pad = -(-n_proj // in_tile) * in_tile
    col_scale = np.ones((1, n_proj_pad), np.float32)
    da_w = DA_HEADS * DA_HEAD_V
    col_scale[0, GLA_PROJ_WIDTH:GLA_PROJ_WIDTH + da_w] = DA_HEAD_QK ** -0.5 * LOG2E
    col_scale[0, :GLA_HEADS * GLA_HEAD_K] = GLA_HEAD_K ** -0.5
    col_scale = jnp.asarray(col_scale)

    w_in_b = w_in.astype(BF16)
    w_in_b = jnp.concatenate([w_in_b[:, :, 3 * da_w:3 * da_w + GLA_PROJ_WIDTH], w_in_b[:, :, :3 * da_w],
                              w_in_b[:, :, 3 * da_w + GLA_PROJ_WIDTH:]], axis=2)
    w_in_b = jnp.pad(w_in_b, ((0, 0), (0, 0), (0, n_proj_pad - n_proj))).reshape(depth * d, -1)
    w_o_b = w_o.astype(BF16).reshape(-1, d)
    w_up_b = w_up.astype(BF16).reshape(depth * d, -1)
    w_down_b = w_down.astype(BF16).reshape(-1, d)

    mod_all = _modulation(c, w_ada, b_ada).reshape(depth, b, N_MOD, d)
    x2d = x.reshape(b * t, d)
    for l in range(depth):
        mod = mod_all[l]
        wup_l = jnp.pad(w_gk_up[l].astype(BF16), ((0, LANE - GLA_GATE_RANK), (0, 0)))

        proj = _in_projection(x2d, mod, g_mix_norm[l], w_in_b, l, col_scale, t, tn=in_tile)
        proj = proj.reshape(b, t, n_proj_pad)
        od = _diff_attention(proj, diff_lambda[l], g_diff_subln[l], l)
        og = _gla(proj, wup_l, b_gk[l], g_gla_norm[l])
        x1, h2 = _out_projection(od.reshape(b * t, -1), og.reshape(b * t, -1), w_o_b, l,
                                 x2d, mod, g_ffn_norm[l], t)
        x2d = _ffn(h2, w_up_b, w_conv[l], b_conv[l], w_down_b, l, x1, mod,
                   g_final, t, final=(l == depth - 1))
    return x2d.reshape(b, t, d)
```

```python
import functools
import math

import numpy as np
import jax
import jax.numpy as jnp
from jax import lax
from jax.experimental import pallas as pl
from jax.experimental.pallas import tpu as pltpu

BF16 = jnp.bfloat16
F32 = jnp.float32

NORM_EPS = 1e-6
N_MOD = 6
DA_HEADS = 8
DA_HEAD_V = 128
DA_HEAD_QK = 64
GLA_HEADS = 4
GLA_HEAD_K = 128
GLA_HEAD_V = 256
GLA_GATE_RANK = 16
GLA_GATE_NORMALIZER = 16.0
CONV_WIDTH = 3

LANE = 128
V7X_VMEM_LIMIT = 56 * 1024 * 1024
LOG2E = 1.4426950408889634
SQRT_HALF = 0.7071067811865476

_NT = (((1,), (1,)), ((), ()))
_TN = (((0,), (0,)), ((), ()))


def _cparams(sem):
    return pltpu.CompilerParams(dimension_semantics=sem, vmem_limit_bytes=V7X_VMEM_LIMIT)


def _lambda_init(layer_idx):
    return 0.8 - 0.6 * math.exp(-0.3 * layer_idx)


def _rms(x, g):
    ms = jnp.mean(x * x, axis=-1, keepdims=True)
    return x * lax.rsqrt(ms + NORM_EPS) * g


def _mod_kernel(c_ref, w_ref, b_ref, o_ref):
    c = c_ref[...]
    ca = (c * jax.nn.sigmoid(c)).astype(BF16)
    w = w_ref[0].astype(BF16)
    o_ref[0] = jnp.dot(ca, w, preferred_element_type=F32) + b_ref[0]


def _modulation(c, w_ada, b_ada, tn=1024):
    depth, d, n = w_ada.shape
    b = c.shape[0]
    return pl.pallas_call(
        _mod_kernel,
        out_shape=jax.ShapeDtypeStruct((depth, b, n), F32),
        grid=(depth, n // tn),
        in_specs=[
            pl.BlockSpec((b, d), lambda l, j: (0, 0)),
            pl.BlockSpec((1, d, tn), lambda l, j: (l, 0, j)),
            pl.BlockSpec((1, 1, tn), lambda l, j: (l, 0, j)),
        ],
        out_specs=pl.BlockSpec((1, b, tn), lambda l, j: (l, 0, j)),
        compiler_params=_cparams(("parallel", "parallel")),
        name="adaln_mod",
    )(c, w_ada, b_ada.reshape(depth, 1, n))


def _inproj_kernel(x_ref, mod_ref, g_ref, w_ref, cs_ref, o_ref, h_ref, *, nchunk):
    i = pl.program_id(0)
    j = pl.program_id(1)
    cr = x_ref.shape[0] // nchunk

    def norm_chunk():
        c = jnp.minimum(j, nchunk - 1)
        rows = pl.ds(pl.multiple_of(c * cr, cr), cr)
        y = _rms(x_ref[rows, :], g_ref[...])
        h_ref[i % 2, rows, :] = (y * (1.0 + mod_ref[0, 1:2, :]) + mod_ref[0, 0:1, :]).astype(BF16)

    @pl.when(i == 0)
    def _():
        norm_chunk()

    @pl.when(i > 0)
    def _():
        acc = jnp.dot(h_ref[(i + 1) % 2], w_ref[...], preferred_element_type=F32)
        o_ref[...] = (acc * cs_ref[...]).astype(BF16)
        norm_chunk()


def _in_projection(x2d, mod, g, w, layer, col_scale, seq, tm=1024, tn=1280):
    n, d = x2d.shape
    p = w.shape[1]
    tm = min(tm, seq)
    tiles_per_seq = seq // tm
    nt = n // tm
    nj = p // tn
    nchunk = max(c for c in range(1, nj + 1) if tm % (16 * c) == 0)
    col = lambda i, j: jnp.where(i == 0, 0, j)
    return pl.pallas_call(
        functools.partial(_inproj_kernel, nchunk=nchunk),
        out_shape=jax.ShapeDtypeStruct((n, p), BF16),
        grid=(nt + 1, nj),
        in_specs=[
            pl.BlockSpec((tm, d), lambda i, j: (jnp.minimum(i, nt - 1), 0)),
            pl.BlockSpec((1, N_MOD, d), lambda i, j: (jnp.minimum(i, nt - 1) // tiles_per_seq, 0, 0)),
            pl.BlockSpec((1, d), lambda i, j: (0, 0)),
            pl.BlockSpec((d, tn), lambda i, j: (layer, col(i, j))),
            pl.BlockSpec((1, tn), lambda i, j: (0, col(i, j))),
        ],
        out_specs=pl.BlockSpec((tm, tn), lambda i, j: (jnp.maximum(i - 1, 0), col(i, j))),
        scratch_shapes=[pltpu.VMEM((2, tm, d), BF16)],
        compiler_params=_cparams(("arbitrary", "arbitrary")),
        name="norm_in_proj",
    )(x2d, mod, g.reshape(1, d), w, col_scale)


def _alibi_slopes(n):
    start = 2.0 ** (-8.0 / n)
    return np.array([start ** (i + 1) for i in range(n)], dtype=np.float32)


def _bf16_split(x):
    hi = x.astype(BF16).astype(np.float32)
    lo = (x - hi).astype(BF16).astype(np.float32)
    return hi, lo


def _alibi_aug(t, blk):
    j = np.arange(t)
    kaug = np.zeros((t, LANE), np.float32)
    kaug[:, 0] = kaug[:, 2] = j % 256
    kaug[:, 1] = kaug[:, 3] = j // 256
    kaug[:, 4] = 1.0
    c = _alibi_slopes(DA_HEADS).astype(np.float64) * LOG2E
    c_hi, c_lo = _bf16_split(c.astype(np.float32))
    qaug = np.zeros((DA_HEADS, t // blk, LANE), np.float32)
    qaug[:, :, 0] = c_hi[:, None]
    qaug[:, :, 1] = 256.0 * c_hi[:, None]
    qaug[:, :, 2] = c_lo[:, None]
    qaug[:, :, 3] = 256.0 * c_lo[:, None]
    qaug[:, :, 4] = -(c[:, None] * (np.arange(t // blk)[None, :] * blk))
    return jnp.asarray(kaug.astype(BF16)), jnp.asarray(qaug.astype(BF16))


def _dattn_kernel(q_ref, k_ref, v_ref, kaug_ref, qaug_ref, lamp_ref, gsub_ref, o_ref, *, blk, lam_init):
    t = q_ref.shape[1]
    nq = t // blk
    heads = q_ref.shape[2] // LANE
    kaug = kaug_ref[...]
    ones = jnp.ones((t, LANE), BF16)

    lamp = lamp_ref[...]
    s1 = jnp.sum(lamp[0:1] * lamp[1:2], axis=-1, keepdims=True)
    s2 = jnp.sum(lamp[2:3] * lamp[3:4], axis=-1, keepdims=True)
    lam = jnp.exp(s1) - jnp.exp(s2) + lam_init

    lane = lax.broadcasted_iota(jnp.int32, (blk, LANE), 1)
    ri = lax.broadcasted_iota(jnp.int32, (2 * blk, blk), 0)
    ci = lax.broadcasted_iota(jnp.int32, (2 * blk, blk), 1)
    causal = (ri & (blk - 1)) >= ci
    zero = jnp.zeros((blk, LANE), BF16)

    def scores(h, i):
        hs = slice(h * LANE, (h + 1) * LANE)
        qi = q_ref[0, i * blk:(i + 1) * blk, hs]
        aug = jnp.broadcast_to(qaug_ref[h, i:i + 1, :], (blk, LANE))
        q1 = jnp.concatenate([jnp.where(lane < DA_HEAD_QK, qi, zero), aug], axis=1)
        q2 = jnp.concatenate([jnp.where(lane >= DA_HEAD_QK, qi, zero), aug], axis=1)
        qc = jnp.concatenate([q1, q2], axis=0)
        lo = i * blk
        kd = jnp.concatenate([k_ref[0, lo:lo + blk, hs], kaug[lo:lo + blk]], axis=1)
        sd = lax.dot_general(qc, kd, _NT, preferred_element_type=F32)
        sm = None
        if i > 0:
            km = jnp.concatenate([k_ref[0, :lo, hs], kaug[:lo]], axis=1)
            sm = lax.dot_general(qc, km, _NT, preferred_element_type=F32)
        return sd, sm

    def finish(h, i, sd, sm):
        hs = slice(h * LANE, (h + 1) * LANE)
        lo = i * blk
        sd = jnp.where(causal, sd, -jnp.inf)
        m = jnp.max(sd, axis=-1, keepdims=True)
        if i > 0:
            m = jnp.maximum(m, jnp.max(sm, axis=-1, keepdims=True))
        vd = jnp.concatenate([v_ref[0, lo:lo + blk, hs], ones[:blk]], axis=1)
        o = jnp.dot(jnp.exp2(sd - m).astype(BF16), vd, preferred_element_type=F32)
        if i > 0:
            vm = jnp.concatenate([v_ref[0, :lo, hs], ones[:lo]], axis=1)
            o = o + jnp.dot(jnp.exp2(sm - m).astype(BF16), vm, preferred_element_type=F32)
        on = o[:, :DA_HEAD_V] * (1.0 / o[:, DA_HEAD_V:DA_HEAD_V + 1])
        od = on[:blk] - lam * on[blk:]
        od = _rms(od, gsub_ref[...]) * (1.0 - lam_init)
        o_ref[0, lo:lo + blk, hs] = od.astype(BF16)

    work = [(h, i) for h in range(heads) for i in range(nq)]
    pending = scores(*work[0])
    for n, hi in enumerate(work):
        nxt = scores(*work[n + 1]) if n + 1 < len(work) else None
        finish(*hi, *pending)
        pending = nxt


def _diff_attention(proj, lamp, gsub, layer_idx, blk=256, heads_per_step=4):
    b, t, _ = proj.shape
    kaug, qaug = _alibi_aug(t, blk)
    hps = heads_per_step
    w = hps * LANE
    ngrp = DA_HEADS // hps
    kern = functools.partial(_dattn_kernel, blk=blk, lam_init=_lambda_init(layer_idx))
    return pl.pallas_call(
        kern,
        out_shape=jax.ShapeDtypeStruct((b, t, DA_HEADS * DA_HEAD_V), BF16),
        grid=(b, ngrp),
        in_specs=[
            pl.BlockSpec((1, t, w), lambda bi, h: (bi, 0, h)),
            pl.BlockSpec((1, t, w), lambda bi, h: (bi, 0, ngrp + h)),
            pl.BlockSpec((1, t, w), lambda bi, h: (bi, 0, 2 * ngrp + h)),
            pl.BlockSpec((t, LANE), lambda bi, h: (0, 0)),
            pl.BlockSpec((hps, t // blk, LANE), lambda bi, h: (h, 0, 0)),
            pl.BlockSpec((4, DA_HEAD_QK), lambda bi, h: (0, 0)),
            pl.BlockSpec((1, DA_HEAD_V), lambda bi, h: (0, 0)),
        ],
        out_specs=pl.BlockSpec((1, t, w), lambda bi, h: (bi, 0, h)),
        compiler_params=_cparams(("parallel", "parallel")),
        name="diff_attention",
    )(proj, proj, proj, kaug, qaug, lamp, gsub.reshape(1, DA_HEAD_V))


GLA_FINE_LEVELS = 3


def _gla_cum_matrices(c):
    nlev = int(math.log2(c))
    nfine = min(GLA_FINE_LEVELS, nlev)
    m = np.zeros((nfine + 1, c, c), np.float32)
    for n, lev in enumerate(range(nlev - nfine, nlev)):
        h = c >> (lev + 1)
        for r in range(c):
            mid = (r // (2 * h)) * 2 * h + h
            if r % (2 * h) >= h:
                m[n, r, mid:r + 1] = 1.0
            else:
                m[n, r, r + 1:mid] = 1.0
    for r in range(c):
        m[nfine, r, :r + 1] = 1.0
    return jnp.asarray(m.astype(BF16))


def _gla_kernel(gq_ref, gk_ref, gv_ref, gg_ref, glow_ref, wup_ref, bgk_ref, gnorm_ref, cum_ref,
                o_ref, s_ref, qt_ref, kt_ref, *, c, first_block=True):
    nlev = int(math.log2(c))
    dk, dv, nh = GLA_HEAD_K, GLA_HEAD_V, GLA_HEADS

    if first_block:
        @pl.when(pl.program_id(1) == 0)
        def _():
            s_ref[...] = jnp.zeros_like(s_ref)

    x = jnp.dot(glow_ref[0], wup_ref[...], preferred_element_type=F32) + bgk_ref[...]
    logsig = jnp.minimum(x, 0.0) - jnp.log(1.0 + jnp.exp(-jnp.abs(x)))
    la = logsig * (LOG2E / GLA_GATE_NORMALIZER)
    la_hi = la.astype(BF16)
    la_mid = (la - la_hi.astype(F32)).astype(BF16)
    la_lo = (la - la_hi.astype(F32) - la_mid.astype(F32)).astype(BF16)

    def cum(idx, parts):
        mat = cum_ref[idx]
        return sum(jnp.dot(mat, part, preferred_element_type=F32) for part in parts)

    nfine = cum_ref.shape[0] - 1
    sub = 8
    kw = nh * dk
    b3 = cum(nfine, (la_hi, la_mid, la_lo)).reshape(c // sub, sub, kw)

    def last_row(tile, ntiles):
        return jnp.broadcast_to(b3[tile, sub - 1:sub, :][None], (ntiles, sub, kw))

    q = gq_ref[0].astype(F32)
    k = gk_ref[0].astype(F32)
    q3 = q.reshape(c // sub, sub, kw)
    k3 = k.reshape(c // sub, sub, kw)
    sl = lax.broadcasted_iota(jnp.int32, (1, sub, kw), 1)

    for lev in range(nlev):
        h = c >> (lev + 1)
        if h >= sub:
            ht = h // sub
            zero = jnp.zeros((ht, sub, kw), F32)
            qparts, kparts = [], []
            for g in range(c // (2 * h)):
                s_rows = slice(2 * ht * g, 2 * ht * g + ht)
                t_rows = slice(2 * ht * g + ht, 2 * ht * (g + 1))
                mid = last_row(2 * ht * g + ht - 1, ht)
                qparts += [zero, q3[t_rows] * jnp.exp2(b3[t_rows] - mid)]
                kparts += [k3[s_rows] * jnp.exp2(mid - b3[s_rows]), zero]
            qt = jnp.concatenate(qparts, axis=0)
            kt = jnp.concatenate(kparts, axis=0)
        else:
            p = jnp.exp2(cum(lev - (nlev - nfine), (la_hi, la_mid))).reshape(c // sub, sub, kw)
            tside = ((sl >> (nlev - 1 - lev)) & 1) == 1
            qt = q3 * jnp.where(tside, p, 0.0)
            kt = k3 * jnp.where(tside, 0.0, p)
        qt_ref[lev] = qt.reshape(c, kw).astype(BF16)
        kt_ref[lev] = kt.reshape(c, kw).astype(BF16)

    qb = (q * jnp.exp2(b3.reshape(c, kw))).astype(BF16)
    kr = (k * jnp.exp2((last_row(c // sub - 1, c // sub) - b3).reshape(c, kw))).astype(BF16)
    qk = q * k

    tx = lax.broadcasted_iota(jnp.int32, (c, c), 0) ^ lax.broadcasted_iota(jnp.int32, (c, c), 1)
    ones = jnp.ones((c, LANE), BF16)

    def level_scores(hh):
        ks = slice(hh * dk, (hh + 1) * dk)
        return [lax.dot_general(qt_ref[lev, :, ks], kt_ref[lev, :, ks], _NT, preferred_element_type=F32)
                for lev in range(nlev)]

    pending = level_scores(0)
    for hh in range(nh):
        ks = slice(hh * dk, (hh + 1) * dk)
        vs = slice(hh * dv, (hh + 1) * dv)
        als = pending
        pending = level_scores(hh + 1) if hh + 1 < nh else None
        a = als[0]
        for lev in range(1, nlev):
            a = jnp.where(tx < (1 << (nlev - lev)), als[lev], a)
        v = gv_ref[0, :, vs]
        s_old = s_ref[hh]
        diag = jnp.sum(qk[:, ks], axis=-1, keepdims=True)
        o = (jnp.dot(a.astype(BF16), v, preferred_element_type=F32)
             + jnp.dot(qb[:, ks], s_old.astype(BF16), preferred_element_type=F32)
             + diag * v.astype(F32))
        tot = (lax.dot_general(la_hi[:, ks], ones, _TN, preferred_element_type=F32)
               + lax.dot_general(la_mid[:, ks], ones, _TN, preferred_element_type=F32))
        dec = jnp.exp2(tot)
        dec = jnp.concatenate([dec] * (dv // LANE), axis=1)
        s_ref[hh] = dec * s_old + lax.dot_general(kr[:, ks], v, _TN, preferred_element_type=F32)
        on = _rms(o, gnorm_ref[...])
        g = gg_ref[0, :, vs].astype(F32)
        o_ref[0, :, vs] = (on * (g * jax.nn.sigmoid(g))).astype(BF16)


def _gla_step_kernel(gq_ref, gk_ref, gv_ref, gg_ref, glow_ref, wup_ref, bgk_ref, gnorm_ref, cum_ref,
                     o_ref, s_ref, qt_ref, kt_ref, *, c):
    for sb in range(o_ref.shape[1] // c):
        rows = slice(sb * c, (sb + 1) * c)
        act = [r.at[:, rows, :] for r in (gq_ref, gk_ref, gv_ref, gg_ref, glow_ref)]
        _gla_kernel(*act, wup_ref, bgk_ref, gnorm_ref, cum_ref, o_ref.at[:, rows, :], s_ref, qt_ref, kt_ref,
                    c=c, first_block=(sb == 0))


def _gla(proj, wup, bgk, gnorm, c=256, blocks_per_step=2):
    b, t, _ = proj.shape
    nlev = int(math.log2(c))
    kw = GLA_HEADS * GLA_HEAD_K
    vw = GLA_HEADS * GLA_HEAD_V
    q_off = 3 * DA_HEADS * DA_HEAD_V
    cum = _gla_cum_matrices(c)
    rows = c * min(blocks_per_step, t // c)
    kern = functools.partial(_gla_step_kernel, c=c)
    return pl.pallas_call(
        kern,
        out_shape=jax.ShapeDtypeStruct((b, t, vw), BF16),
        grid=(b, t // rows),
        in_specs=[
            pl.BlockSpec((1, rows, kw), lambda bi, i: (bi, i, q_off // kw)),
            pl.BlockSpec((1, rows, kw), lambda bi, i: (bi, i, q_off // kw + 1)),
            pl.BlockSpec((1, rows, vw), lambda bi, i: (bi, i, (q_off + 2 * kw) // vw)),
            pl.BlockSpec((1, rows, vw), lambda bi, i: (bi, i, (q_off + 2 * kw) // vw + 1)),
            pl.BlockSpec((1, rows, LANE), lambda bi, i: (bi, i, (q_off + 2 * kw + 2 * vw) // LANE)),
            pl.BlockSpec((LANE, kw), lambda bi, i: (0, 0)),
            pl.BlockSpec((1, kw), lambda bi, i: (0, 0)),
            pl.BlockSpec((1, GLA_HEAD_V), lambda bi, i: (0, 0)),
            pl.BlockSpec(cum.shape, lambda bi, i: (0, 0, 0)),
        ],
        out_specs=pl.BlockSpec((1, rows, vw), lambda bi, i: (bi, i, 0)),
        scratch_shapes=[
            pltpu.VMEM((GLA_HEADS, GLA_HEAD_K, GLA_HEAD_V), F32),
            pltpu.VMEM((nlev, c, kw), BF16),
            pltpu.VMEM((nlev, c, kw), BF16),
        ],
        compiler_params=_cparams(("parallel", "arbitrary")),
        name="gla",
    )(proj, proj, proj, proj, proj, wup, bgk.reshape(1, kw), gnorm.reshape(1, GLA_HEAD_V), cum)


def _outproj_kernel(od_ref, og_ref, w1_ref, w2_ref, x_ref, mod_ref, g_ref, x1_ref, h2_ref):
    mix = (jnp.dot(od_ref[...], w1_ref[...], preferred_element_type=F32)
           + jnp.dot(og_ref[...], w2_ref[...], preferred_element_type=F32))
    x1 = x_ref[...] + mod_ref[0, 2:3, :] * mix
    x1_ref[...] = x1
    y = _rms(x1, g_ref[...])
    h2_ref[...] = (y * (1.0 + mod_ref[0, 4:5, :]) + mod_ref[0, 3:4, :]).astype(BF16)


def _out_projection(od, og, w_o, layer, x2d, mod, g, seq, tm=512):
    n, d = x2d.shape
    half = od.shape[1]
    tiles_per_seq = seq // tm
    return pl.pallas_call(
        _outproj_kernel,
        out_shape=(jax.ShapeDtypeStruct((n, d), F32), jax.ShapeDtypeStruct((n, d), BF16)),
        grid=(n // tm,),
        in_specs=[
            pl.BlockSpec((tm, half), lambda i: (i, 0)),
            pl.BlockSpec((tm, half), lambda i: (i, 0)),
            pl.BlockSpec((half, d), lambda i: (2 * layer, 0)),
            pl.BlockSpec((half, d), lambda i: (2 * layer + 1, 0)),
            pl.BlockSpec((tm, d), lambda i: (i, 0)),
            pl.BlockSpec((1, N_MOD, d), lambda i: (i // tiles_per_seq, 0, 0)),
            pl.BlockSpec((1, d), lambda i: (0, 0)),
        ],
        out_specs=(pl.BlockSpec((tm, d), lambda i: (i, 0)), pl.BlockSpec((tm, d), lambda i: (i, 0))),
        compiler_params=_cparams(("parallel",)),
        name="out_proj_residual_norm",
    )(od, og, w_o, w_o, x2d, mod, g.reshape(1, d))


def _ffn_kernel(h2_ref, wu_ref, wg_ref, wc_ref, bc_ref, wd_ref, x1_hbm, mod_ref, gfin_ref,
                o_ref, x1_buf, x1_sem, carry_ref, *, tiles_per_seq, final):
    i = pl.program_id(0)
    j = pl.program_id(1)
    last = pl.num_programs(1) - 1
    tm = h2_ref.shape[0]
    tf = wu_ref.shape[1]
    sub = 8

    def x1_copy():
        return pltpu.make_async_copy(x1_hbm.at[pl.ds(pl.multiple_of(i * tm, tm), tm), :], x1_buf, x1_sem)

    @pl.when(j == 0)
    def _():
        x1_copy().start()

    def down_partial():
        h2 = h2_ref[...]
        g = jnp.dot(h2, wg_ref[...], preferred_element_type=F32)
        u = jnp.dot(h2, wu_ref[...], preferred_element_type=F32)
        g3 = g.reshape(tm // sub, sub, tf)

        prev = carry_ref[j]
        prev = jnp.where(i % tiles_per_seq == 0, jnp.zeros_like(prev), prev)
        carry_ref[j] = g3[tm // sub - 1]
        sl = lax.broadcasted_iota(jnp.int32, (1, sub, tf), 1)

        def shifted(n):
            rot = pltpu.roll(g3, n, 1)
            before = jnp.concatenate([pltpu.roll(prev[None], n, 1), rot[:-1]], axis=0)
            return jnp.where(sl < n, before, rot)

        wc = wc_ref[...] * SQRT_HALF
        z = bc_ref[...] * SQRT_HALF + shifted(2) * wc[0:1] + shifted(1) * wc[1:2] + g3 * wc[2:3]
        act = (z * (1.0 + lax.erf(z))).reshape(tm, tf) * u
        return jnp.dot(act.astype(BF16), wd_ref[...], preferred_element_type=F32)

    @pl.when(j == 0)
    def _():
        o_ref[...] = down_partial()

    @pl.when(j > 0)
    def _():
        o_ref[...] += down_partial()

    @pl.when(j == last)
    def _():
        x1_copy().wait()
        x2 = x1_buf[...] + (mod_ref[0, 5:6, :] * SQRT_HALF) * o_ref[...]
        if final:
            x2 = _rms(x2, gfin_ref[...])
        o_ref[...] = x2


def _ffn(h2, w_up, w_conv, b_conv, w_down, layer, x1, mod, g_final, seq, final, tm=1024, tf=512):
    n, d = x1.shape
    f = w_conv.shape[1]
    nf = f // tf
    tm = min(tm, seq)
    tiles_per_seq = seq // tm
    kern = functools.partial(_ffn_kernel, tiles_per_seq=tiles_per_seq, final=final)
    return pl.pallas_call(
        kern,
        out_shape=jax.ShapeDtypeStruct((n, d), F32),
        grid=(n // tm, nf),
        in_specs=[
            pl.BlockSpec((tm, d), lambda i, j: (i, 0)),
            pl.BlockSpec((d, tf), lambda i, j: (layer, j)),
            pl.BlockSpec((d, tf), lambda i, j: (layer, nf + j)),
            pl.BlockSpec((CONV_WIDTH, tf), lambda i, j: (0, j)),
            pl.BlockSpec((1, tf), lambda i, j: (0, j)),
            pl.BlockSpec((tf, d), lambda i, j: (layer * nf + j, 0)),
            pl.BlockSpec(memory_space=pl.ANY),
            pl.BlockSpec((1, N_MOD, d), lambda i, j: (i // tiles_per_seq, 0, 0)),
            pl.BlockSpec((1, d), lambda i, j: (0, 0)),
        ],
        out_specs=pl.BlockSpec((tm, d), lambda i, j: (i, 0)),
        scratch_shapes=[
            pltpu.VMEM((tm, d), F32),
            pltpu.SemaphoreType.DMA,
            pltpu.VMEM((nf, 8, tf), F32),
        ],
        compiler_params=_cparams(("arbitrary", "arbitrary")),
        name="conv_ffn_residual",
    )(h2, w_up, w_up, w_conv, b_conv.reshape(1, f), w_down, x1, mod, g_final.reshape(1, d))


def kernel(x, c, w_ada, b_ada, g_mix_norm, w_in, diff_lambda, g_diff_subln, w_gk_up, b_gk,
           g_gla_norm, w_o, g_ffn_norm, w_up, w_conv, b_conv, w_down, g_final):
    b, t, d = x.shape
    depth = w_ada.shape[0]
    n_proj = w_in.shape[2]
    in_tile = 1280
    n_proj_pad = -(-n_proj // in_tile) * in_tile
    col_scale = np.ones((1, n_proj_pad), np.float32)
    da_w = DA_HEADS * DA_HEAD_V
    col_scale[0, :da_w] = DA_HEAD_QK ** -0.5 * LOG2E
    col_scale[0, 3 * da_w:3 * da_w + GLA_HEADS * GLA_HEAD_K] = GLA_HEAD_K ** -0.5
    col_scale = jnp.asarray(col_scale)

    w_in_b = jnp.pad(w_in.astype(BF16), ((0, 0), (0, 0), (0, n_proj_pad - n_proj))).reshape(depth * d, -1)
    w_o_b = w_o.astype(BF16).reshape(-1, d)
    w_up_b = w_up.astype(BF16).reshape(depth * d, -1)
    w_down_b = w_down.astype(BF16).reshape(-1, d)

    mod_all = _modulation(c, w_ada, b_ada).reshape(depth, b, N_MOD, d)
    x2d = x.reshape(b * t, d)
    for l in range(depth):
        mod = mod_all[l]
        wup_l = jnp.pad(w_gk_up[l].astype(BF16), ((0, LANE - GLA_GATE_RANK), (0, 0)))

        proj = _in_projection(x2d, mod, g_mix_norm[l], w_in_b, l, col_scale, t, tn=in_tile)
        proj = proj.reshape(b, t, n_proj_pad)
        od = _diff_attention(proj, diff_lambda[l], g_diff_subln[l], l)
        og = _gla(proj, wup_l, b_gk[l], g_gla_norm[l])
        x1, h2 = _out_projection(od.reshape(b * t, -1), og.reshape(b * t, -1), w_o_b, l,
                                 x2d, mod, g_ffn_norm[l], t)
        x2d = _ffn(h2, w_up_b, w_conv[l], b_conv[l], w_down_b, l, x1, mod,
                   g_final, t, final=(l == depth - 1))
    return x2d.reshape(b, t, d)
```
